```python
import math
import jax, jax.numpy as jnp
from jax import lax
import numpy as np

D_MODEL = 2048
BATCH = 4
SEQ = 8192
DEPTH = 4

CHUNK = 64
N_MIXERS = 3
DN_ALPHA = (2.0 * DEPTH) ** 0.25
DN_BETA = (8.0 * DEPTH) ** -0.25
LN_EPS = 1e-5
POOL_WINDOWS = (2, 4, 8, 16)
POOL_GROUPS = len(POOL_WINDOWS)
POOL_GROUP_DIM = D_MODEL // POOL_GROUPS
POOL_MAX_WINDOW = max(POOL_WINDOWS)
SSM_GROUP_DIM = 16
SSM_GROUPS = D_MODEL // SSM_GROUP_DIM
SSM_STATE = 64
SSM_DT_MIN = 1e-3
SSM_DT_MAX = 1e-1
ATTN_HEADS = 16
ATTN_HEAD_DIM = D_MODEL // ATTN_HEADS
ATTN_LEFT_CHUNKS = 8
ATTN_LEFT = ATTN_LEFT_CHUNKS * CHUNK
ATTN_BAND = (ATTN_LEFT_CHUNKS + 1) * CHUNK
ATTN_MAX_REL = 2 * CHUNK
NEG_INF = -1e30
PEER_N_KEYS = 128
PEER_EXPERTS = PEER_N_KEYS * PEER_N_KEYS
PEER_HEADS = 8
PEER_QUERY_DIM = 256
PEER_HALF = PEER_QUERY_DIM // 2
PEER_TOPK = 16
PEER_TOKEN_BLOCK = 128
PLE_DIM = 256
N_POOL_LAYERS = len(range(0, DEPTH, N_MIXERS))
N_SSM_LAYERS = len(range(1, DEPTH, N_MIXERS))
N_ATTN_LAYERS = len(range(2, DEPTH, N_MIXERS))

kernel_name = 'hybrid_pool_s5_chunkattn_peer_deepnorm'


def layer_norm(x, g, b):
    xf = x.astype(jnp.float32)
    mu = jnp.mean(xf, axis=-1, keepdims=True)
    var = jnp.mean(jnp.square(xf - mu), axis=-1, keepdims=True)
    y = (xf - mu) * lax.rsqrt(var + LN_EPS) * g.astype(jnp.float32) + b.astype(jnp.float32)
    return y.astype(x.dtype)


def pool_mixer(h, w_in, w_group, scale, w_out):
    bsz, seq, _ = h.shape
    u = (h @ w_in).astype(jnp.float32)
    csum = jnp.pad(jnp.cumsum(u, axis=1), ((0, 0), (POOL_MAX_WINDOW, 0), (0, 0)))
    t = jnp.arange(seq, dtype=jnp.float32)
    outs = []
    for g, w in enumerate(POOL_WINDOWS):
        sl = slice(g * POOL_GROUP_DIM, (g + 1) * POOL_GROUP_DIM)
        upper = csum[:, POOL_MAX_WINDOW:, sl]
        lower = csum[:, POOL_MAX_WINDOW - w:POOL_MAX_WINDOW - w + seq, sl]
        count = jnp.minimum(t + 1.0, float(w))[None, :, None]
        outs.append((upper - lower) / count - u[:, :, sl])
    pooled = jnp.stack(outs, axis=2)
    mixed = jnp.einsum('bsgc,gcd->bsgd', pooled, w_group.astype(jnp.float32)).reshape(bsz, seq, D_MODEL)
    mixed = (mixed * scale.astype(jnp.float32)).astype(h.dtype)
    return mixed @ w_out


def _cmul(ar, ai, br, bi):
    return ar * br - ai * bi, ar * bi + ai * br


def _ssm_combine(e1, e2):
    a1r, a1i, b1r, b1i = e1
    a2r, a2i, b2r, b2i = e2
    ar, ai = _cmul(a2r, a2i, a1r, a1i)
    br, bi = _cmul(a2r, a2i, b1r, b1i)
    return ar, ai, br + b2r, bi + b2i


def ssm_mixer(h, w_in, lam_re, lam_im, log_step, b_re, b_im, c_re, c_im, d_skip, w_glu, w_out):
    bsz, seq, _ = h.shape
    n_chunks = seq // CHUNK
    f32 = jnp.float32
    u = (h @ w_in).astype(f32).reshape(bsz, n_chunks, CHUNK, SSM_GROUPS, SSM_GROUP_DIM)
    u = jnp.moveaxis(u, 1, 0)
    lr, li = lam_re.astype(f32), lam_im.astype(f32)
    dt = jnp.exp(log_step.astype(f32))[:, None]
    mag = jnp.exp(dt * lr)
    abar_r, abar_i = mag * jnp.cos(dt * li), mag * jnp.sin(dt * li)
    den = lr * lr + li * li
    k_r = ((abar_r - 1.0) * lr + abar_i * li) / den
    k_i = (abar_i * lr - (abar_r - 1.0) * li) / den
    bb_r, bb_i = _cmul(k_r[..., None], k_i[..., None], b_re.astype(f32), b_im.astype(f32))
    a_r = jnp.broadcast_to(abar_r, (bsz, CHUNK, SSM_GROUPS, SSM_STATE))
    a_i = jnp.broadcast_to(abar_i, (bsz, CHUNK, SSM_GROUPS, SSM_STATE))
    cr, ci = c_re.astype(f32), c_im.astype(f32)
    dd = d_skip.astype(f32).reshape(SSM_GROUPS, SSM_GROUP_DIM)

    def step(state, u_c):
        s_r, s_i = state
        bu_r = jnp.einsum('blgc,gpc->blgp', u_c, bb_r)
        bu_i = jnp.einsum('blgc,gpc->blgp', u_c, bb_i)
        cum_r, cum_i, loc_r, loc_i = lax.associative_scan(_ssm_combine, (a_r, a_i, bu_r, bu_i), axis=1)
        car_r, car_i = _cmul(cum_r, cum_i, s_r[:, None], s_i[:, None])
        hr, hi = loc_r + car_r, loc_i + car_i
        y = jnp.einsum('gcp,blgp->blgc', cr, hr) - jnp.einsum('gcp,blgp->blgc', ci, hi) + dd * u_c
        return (hr[:, -1], hi[:, -1]), y

    init = (jnp.zeros((bsz, SSM_GROUPS, SSM_STATE), f32), jnp.zeros((bsz, SSM_GROUPS, SSM_STATE), f32))
    _, y = lax.scan(step, init, u)
    y = jnp.moveaxis(y, 0, 1).reshape(bsz, seq, D_MODEL)
    g = jax.nn.gelu(y, approximate=False).astype(h.dtype)
    g = g * jax.nn.sigmoid(g @ w_glu)
    return g @ w_out


def chunk_attention(h, w_qkv, rel_bias, w_out):
    bsz, seq, _ = h.shape
    n_chunks = seq // CHUNK
    qkv = (h @ w_qkv).reshape(bsz, seq, 3, ATTN_HEADS, ATTN_HEAD_DIM)
    q = qkv[:, :, 0] * (ATTN_HEAD_DIM ** -0.5)
    k = jnp.pad(qkv[:, :, 1], ((0, 0), (ATTN_LEFT, 0), (0, 0), (0, 0)))
    v = jnp.pad(qkv[:, :, 2], ((0, 0), (ATTN_LEFT, 0), (0, 0), (0, 0)))
    qi = jnp.arange(CHUNK)[:, None]
    kj = jnp.arange(ATTN_BAND)[None, :]
    rel = jnp.clip(qi + ATTN_LEFT - kj, -ATTN_MAX_REL, ATTN_MAX_REL) + ATTN_MAX_REL
    bias = rel_bias.astype(jnp.float32)[:, rel]
    band_offsets = jnp.arange(ATTN_BAND)

    def one_chunk(c):
        start = c * CHUNK
        q_c = lax.dynamic_slice_in_dim(q, start, CHUNK, axis=1)
        k_c = lax.dynamic_slice_in_dim(k, start, ATTN_BAND, axis=1)
        v_c = lax.dynamic_slice_in_dim(v, start, ATTN_BAND, axis=1)
        s = jnp.einsum('bqhd,bkhd->bhqk', q_c, k_c).astype(jnp.float32) + bias
        valid = (start - ATTN_LEFT + band_offsets) >= 0
        s = jnp.where(valid, s, NEG_INF)
        pr = jax.nn.softmax(s, axis=-1).astype(v_c.dtype)
        return jnp.einsum('bhqk,bkhd->bqhd', pr, v_c)

    o = lax.map(one_chunk, jnp.arange(n_chunks))
    o = jnp.moveaxis(o, 0, 1).reshape(bsz, seq, D_MODEL)
    return o @ w_out


def peer_ffn(h, w_q, sub_keys, u_table, v_table):
    bsz, seq, _ = h.shape
    n_tok = bsz * seq
    n_blk = n_tok // PEER_TOKEN_BLOCK
    xt = h.reshape(n_tok, D_MODEL)
    q = (xt @ w_q).astype(jnp.float32).reshape(n_tok, PEER_HEADS, 2, PEER_HALF)
    scores = jnp.einsum('thsk,snk->thsn', q, sub_keys.astype(jnp.float32))
    top_s, top_i = lax.top_k(scores, PEER_TOPK)
    cand = top_s[:, :, 0, :, None] + top_s[:, :, 1, None, :]
    cand = cand.reshape(n_tok, PEER_HEADS, PEER_TOPK * PEER_TOPK)
    best_s, best_c = lax.top_k(cand, PEER_TOPK)
    idx1 = jnp.take_along_axis(top_i[:, :, 0], best_c // PEER_TOPK, axis=-1)
    idx2 = jnp.take_along_axis(top_i[:, :, 1], best_c % PEER_TOPK, axis=-1)
    expert = (idx1 * PEER_N_KEYS + idx2).reshape(n_blk, PEER_TOKEN_BLOCK, PEER_HEADS * PEER_TOPK)
    gate = jax.nn.softmax(best_s, axis=-1).reshape(n_blk, PEER_TOKEN_BLOCK, PEER_HEADS * PEER_TOPK)
    xb = xt.reshape(n_blk, PEER_TOKEN_BLOCK, D_MODEL)

    def token_block(args):
        x_b, e_b, g_b = args
        u_sel = jnp.take(u_table, e_b, axis=0)
        act = jax.nn.gelu(jnp.einsum('td,tkd->tk', x_b, u_sel).astype(jnp.float32), approximate=False) * g_b
        v_sel = jnp.take(v_table, e_b, axis=0)
        return jnp.einsum('tk,tkd->td', act.astype(v_sel.dtype), v_sel)

    y = lax.map(token_block, (xb, expert, gate))
    return y.reshape(bsz, seq, D_MODEL).astype(h.dtype)


def _normal(key, shape, scale):
    return scale * jax.random.normal(key, shape, jnp.float32)


def setup_inputs(seed: int = 0) -> dict:
    key = jax.random.key(seed)
    ks = jax.random.split(key, 40)
    D = D_MODEL
    s_d = D ** -0.5
    qkv_col_scale = jnp.concatenate([jnp.ones((2 * D,), jnp.float32), jnp.full((D,), DN_BETA, jnp.float32)])
    lam_im_base = jnp.pi * jnp.arange(SSM_STATE, dtype=jnp.float32)
    return {
        'x': _normal(ks[0], (BATCH, SEQ, D), 1.0),
        'p': _normal(ks[1], (DEPTH, BATCH, SEQ, PLE_DIM), 1.0),
        'ln_mix_g': 1.0 + _normal(ks[2], (DEPTH, D), 0.02),
        'ln_mix_b': _normal(ks[3], (DEPTH, D), 0.02),
        'ln_ffn_g': 1.0 + _normal(ks[4], (DEPTH, D), 0.02),
        'ln_ffn_b': _normal(ks[5], (DEPTH, D), 0.02),
        'pool_w_in': _normal(ks[6], (N_POOL_LAYERS, D, D), s_d),
        'pool_w_group': _normal(ks[7], (N_POOL_LAYERS, POOL_GROUPS, POOL_GROUP_DIM, POOL_GROUP_DIM), POOL_GROUP_DIM ** -0.5),
        'pool_scale': 1.0 + _normal(ks[8], (N_POOL_LAYERS, D), 0.1),
        'pool_w_out': _normal(ks[9], (N_POOL_LAYERS, D, D), s_d * DN_BETA),
        'ssm_w_in': _normal(ks[10], (N_SSM_LAYERS, D, D), s_d),
        'ssm_lambda_re': -0.5 + _normal(ks[11], (N_SSM_LAYERS, SSM_GROUPS, SSM_STATE), 0.01),
        'ssm_lambda_im': lam_im_base + _normal(ks[12], (N_SSM_LAYERS, SSM_GROUPS, SSM_STATE), 0.01),
        'ssm_log_step': jax.random.uniform(ks[13], (N_SSM_LAYERS, SSM_GROUPS), jnp.float32, math.log(SSM_DT_MIN), math.log(SSM_DT_MAX)),
        'ssm_b_re': _normal(ks[14], (N_SSM_LAYERS, SSM_GROUPS, SSM_STATE, SSM_GROUP_DIM), (2.0 * SSM_GROUP_DIM) ** -0.5),
        'ssm_b_im': _normal(ks[15], (N_SSM_LAYERS, SSM_GROUPS, SSM_STATE, SSM_GROUP_DIM), (2.0 * SSM_GROUP_DIM) ** -0.5),
        'ssm_c_re': _normal(ks[16], (N_SSM_LAYERS, SSM_GROUPS, SSM_GROUP_DIM, SSM_STATE), (2.0 * SSM_STATE) ** -0.5),
        'ssm_c_im': _normal(ks[17], (N_SSM_LAYERS, SSM_GROUPS, SSM_GROUP_DIM, SSM_STATE), (2.0 * SSM_STATE) ** -0.5),
        'ssm_d': _normal(ks[18], (N_SSM_LAYERS, D), 1.0),
        'ssm_w_glu': _normal(ks[19], (N_SSM_LAYERS, D, D), s_d),
        'ssm_w_out': _normal(ks[20], (N_SSM_LAYERS, D, D), s_d * DN_BETA),
        'attn_w_qkv': _normal(ks[21], (N_ATTN_LAYERS, D, 3 * D), s_d) * qkv_col_scale,
        'attn_rel_bias': _normal(ks[22], (N_ATTN_LAYERS, ATTN_HEADS, 2 * ATTN_MAX_REL + 1), 0.2),
        'attn_w_out': _normal(ks[23], (N_ATTN_LAYERS, D, D), s_d * DN_BETA),
        'peer_w_q': _normal(ks[24], (DEPTH, D, PEER_HEADS * PEER_QUERY_DIM), s_d),
        'peer_sub_keys': _normal(ks[25], (DEPTH, 2, PEER_N_KEYS, PEER_HALF), PEER_HALF ** -0.5),
        'peer_u': _normal(ks[26], (DEPTH, PEER_EXPERTS, D), s_d),
        'peer_v': _normal(ks[27], (DEPTH, PEER_EXPERTS, D), DN_BETA),
        'ple_w_proj': _normal(ks[28], (DEPTH, PLE_DIM, D), PLE_DIM ** -0.5),
        'ple_w_gate': _normal(ks[29], (DEPTH, D, D), s_d),
        'ple_b_gate': _normal(ks[30], (DEPTH, D), 0.02),
    }


def reference(x, p, ln_mix_g, ln_mix_b, ln_ffn_g, ln_ffn_b, pool_w_in, pool_w_group, pool_scale, pool_w_out, ssm_w_in, ssm_lambda_re, ssm_lambda_im, ssm_log_step, ssm_b_re, ssm_b_im, ssm_c_re, ssm_c_im, ssm_d, ssm_w_glu, ssm_w_out, attn_w_qkv, attn_rel_bias, attn_w_out, peer_w_q, peer_sub_keys, peer_u, peer_v, ple_w_proj, ple_w_gate, ple_b_gate):
    h = x
    for i in range(DEPTH):
        kind, j = i % N_MIXERS, i // N_MIXERS
        if kind == 0:
            m = pool_mixer(h, pool_w_in[j], pool_w_group[j], pool_scale[j], pool_w_out[j])
        elif kind == 1:
            m = ssm_mixer(h, ssm_w_in[j], ssm_lambda_re[j], ssm_lambda_im[j], ssm_log_step[j], ssm_b_re[j], ssm_b_im[j], ssm_c_re[j], ssm_c_im[j], ssm_d[j], ssm_w_glu[j], ssm_w_out[j])
        else:
            m = chunk_attention(h, attn_w_qkv[j], attn_rel_bias[j], attn_w_out[j])
        h = layer_norm(DN_ALPHA * h + m.astype(h.dtype), ln_mix_g[i], ln_mix_b[i])
        f = peer_ffn(h, peer_w_q[i], peer_sub_keys[i], peer_u[i], peer_v[i])
        h = layer_norm(DN_ALPHA * h + f, ln_ffn_g[i], ln_ffn_b[i])
        h = h + jax.nn.sigmoid(h @ ple_w_gate[i] + ple_b_gate[i]) * (p[i] @ ple_w_proj[i])
    return h
```

```python
import functools
import math

import jax
import jax.numpy as jnp
from jax import lax
from jax.experimental import pallas as pl
from jax.experimental.pallas import tpu as pltpu

F32 = jnp.float32
BF16 = jnp.bfloat16

CHUNK = 64
N_MIXERS = 3
LN_EPS = 1e-5
POOL_WINDOWS = (2, 4, 8, 16)
POOL_MAX_WINDOW = max(POOL_WINDOWS)
SSM_GROUP_DIM = 16
SSM_STATE = 64
ATTN_HEADS = 16
ATTN_LEFT_CHUNKS = 8
ATTN_MAX_REL = 2 * CHUNK
NEG_INF = -1e30
PEER_N_KEYS = 128
PEER_HEADS = 8
PEER_HALF = 128
PEER_TOPK = 16

LANES = 128
VMEM_LIMIT = 56 * 1024 * 1024
ROW_TILE = 256
MM_TM, MM_TN = 512, 1024
ATTN_TQ = ATTN_LEFT_CHUNKS * CHUNK
PEER_TS = 256
PEER_TT = 512
PEER_EB = 256
SSM_LC = 64


def _params(sem):
    return pltpu.CompilerParams(dimension_semantics=sem, vmem_limit_bytes=VMEM_LIMIT)


def _divisor_tile(n, preferred):
    best = LANES
    for cand in range(LANES, min(n, preferred) + 1, LANES):
        if n % cand == 0:
            best = cand
    assert n % best == 0, (n, preferred)
    return best


def _gelu(x):
    return 0.5 * x * (1.0 + lax.erf(x * (1.0 / math.sqrt(2.0))))


def _layer_norm(z, g, b):
    mu = jnp.mean(z, axis=-1, keepdims=True)
    zc = z - mu
    var = jnp.mean(zc * zc, axis=-1, keepdims=True)
    return zc * lax.rsqrt(var + LN_EPS) * g + b


def _mm_body(a_ref, w_ref, o_ref):
    o_ref[...] = jnp.dot(a_ref[...], w_ref[...], preferred_element_type=F32).astype(o_ref.dtype)


def _matmul(a, w, out_dtype):
    m, k = a.shape
    n = w.shape[1]
    tm, tn = _divisor_tile(m, MM_TM), _divisor_tile(n, MM_TN)
    return pl.pallas_call(
        _mm_body,
        grid=(n // tn, m // tm),
        in_specs=[pl.BlockSpec((tm, k), lambda j, i: (i, 0)), pl.BlockSpec((k, tn), lambda j, i: (0, j))],
        out_specs=pl.BlockSpec((tm, tn), lambda j, i: (i, j)),
        out_shape=jax.ShapeDtypeStruct((m, n), out_dtype),
        compiler_params=_params(("arbitrary", "arbitrary")),
        name="matmul",
    )(a, w)


def _mm_ln_body(alpha, a_ref, w_ref, res_ref, g_ref, b_ref, h_ref, ht_ref):
    m = jnp.dot(a_ref[...], w_ref[...], preferred_element_type=F32)
    y = _layer_norm(alpha * res_ref[...] + m, g_ref[...], b_ref[...])
    h_ref[...] = y
    ht_ref[...] = y.T.astype(BF16)


def _matmul_ln(a, w, res, g, b, alpha):
    t, k = a.shape
    d = w.shape[1]
    tm = min(ROW_TILE, t)
    return pl.pallas_call(
        functools.partial(_mm_ln_body, alpha),
        grid=(t // tm,),
        in_specs=[
            pl.BlockSpec((tm, k), lambda i: (i, 0)),
            pl.BlockSpec((k, d), lambda i: (0, 0)),
            pl.BlockSpec((tm, d), lambda i: (i, 0)),
            pl.BlockSpec((1, d), lambda i: (0, 0)),
            pl.BlockSpec((1, d), lambda i: (0, 0)),
        ],
        out_specs=[pl.BlockSpec((tm, d), lambda i: (i, 0)), pl.BlockSpec((d, tm), lambda i: (0, i))],
        out_shape=[jax.ShapeDtypeStruct((t, d), F32), jax.ShapeDtypeStruct((d, t), BF16)],
        compiler_params=_params(("arbitrary",)),
        name="outproj_ln",
    )(a, w, res, g.reshape(1, d), b.reshape(1, d))


def _ple_body(hb_ref, h_ref, p_ref, wg_ref, wp_ref, bg_ref, o_ref, ob_ref):
    gate = jax.nn.sigmoid(jnp.dot(hb_ref[...], wg_ref[...], preferred_element_type=F32) + bg_ref[...])
    proj = jnp.dot(p_ref[...], wp_ref[...], preferred_element_type=F32)
    y = h_ref[...] + gate * proj
    o_ref[...] = y
    ob_ref[...] = y.astype(BF16)


def _ple(hb, h, p, wg, wp, bg):
    t, d = h.shape
    pd = p.shape[1]
    tm = min(ROW_TILE, t)
    return pl.pallas_call(
        _ple_body,
        grid=(t // tm,),
        in_specs=[
            pl.BlockSpec((tm, d), lambda i: (i, 0)),
            pl.BlockSpec((tm, d), lambda i: (i, 0)),
            pl.BlockSpec((tm, pd), lambda i: (i, 0)),
            pl.BlockSpec((d, d), lambda i: (0, 0)),
            pl.BlockSpec((pd, d), lambda i: (0, 0)),
            pl.BlockSpec((1, d), lambda i: (0, 0)),
        ],
        out_specs=[pl.BlockSpec((tm, d), lambda i: (i, 0)), pl.BlockSpec((tm, d), lambda i: (i, 0))],
        out_shape=[jax.ShapeDtypeStruct((t, d), F32), jax.ShapeDtypeStruct((t, d), BF16)],
        compiler_params=_params(("arbitrary",)),
        name="ple_gate",
    )(hb, h, p, wg, wp, bg.reshape(1, d))


def _pool_body(tiles_per_seq, x_ref, win_ref, wg_ref, sc_ref, o_ref, ext_ref):
    wmax = POOL_MAX_WINDOW
    tm = x_ref.shape[0]
    gd = wg_ref.shape[1]
    i = pl.program_id(0)

    @pl.when(i % tiles_per_seq == 0)
    def _():
        ext_ref[0:wmax, :] = jnp.zeros((wmax, ext_ref.shape[1]), F32)

    ext_ref[wmax:wmax + tm, :] = jnp.dot(x_ref[...], win_ref[...], preferred_element_type=F32)
    pos = (i % tiles_per_seq) * tm + lax.broadcasted_iota(jnp.int32, (tm, 1), 0)
    for g, w in enumerate(POOL_WINDOWS):
        cols = slice(g * gd, (g + 1) * gd)
        u = ext_ref[wmax:wmax + tm, cols]
        acc = u
        for k in range(1, w):
            acc = acc + ext_ref[wmax - k:wmax - k + tm, cols]
        count = jnp.minimum(pos + 1, w).astype(F32)
        pooled = acc / count - u
        mixed = jnp.dot(pooled.astype(BF16), wg_ref[g], preferred_element_type=F32) * sc_ref[:, cols]
        o_ref[:, cols] = mixed.astype(BF16)
    ext_ref[0:wmax, :] = ext_ref[tm:tm + wmax, :]


def _pool_mixer(hb, seq, w_in, w_group, scale):
    t, d = hb.shape
    tm = min(ROW_TILE, seq)
    ng, gd = w_group.shape[0], w_group.shape[1]
    return pl.pallas_call(
        functools.partial(_pool_body, seq // tm),
        grid=(t // tm,),
        in_specs=[
            pl.BlockSpec((tm, d), lambda i: (i, 0)),
            pl.BlockSpec((d, d), lambda i: (0, 0)),
            pl.BlockSpec((ng, gd, gd), lambda i: (0, 0, 0)),
            pl.BlockSpec((1, d), lambda i: (0, 0)),
        ],
        out_specs=pl.BlockSpec((tm, d), lambda i: (i, 0)),
        out_shape=jax.ShapeDtypeStruct((t, d), BF16),
        scratch_shapes=[pltpu.VMEM((tm + POOL_MAX_WINDOW, d), F32)],
        compiler_params=_params(("arbitrary",)),
        name="pool_mixer",
    )(hb, w_in, w_group, scale.reshape(1, d))


def _cmul(ar, ai, br, bi):
    return ar * br - ai * bi, ar * bi + ai * br


def _ssm_operators(lam_re, lam_im, log_step, b_re, b_im, c_re, c_im, d_skip, lc, n_blocks):
    f32 = F32
    lr, li = lam_re.astype(f32), lam_im.astype(f32)
    dt = jnp.exp(log_step.astype(f32))[:, None]
    mag = jnp.exp(dt * lr)
    abar_r, abar_i = mag * jnp.cos(dt * li), mag * jnp.sin(dt * li)
    den = lr * lr + li * li
    k_r = ((abar_r - 1.0) * lr + abar_i * li) / den
    k_i = (abar_i * lr - (abar_r - 1.0) * li) / den
    bb_r, bb_i = _cmul(k_r[..., None], k_i[..., None], b_re.astype(f32), b_im.astype(f32))
    cr, ci = c_re.astype(f32), c_im.astype(f32)
    g_n, p_n = lr.shape
    c_n = bb_r.shape[-1]

    def pw_step(carry, _):
        pr, pi = carry
        nr, ni = _cmul(pr, pi, abar_r, abar_i)
        return (nr, ni), (pr, pi)

    (_, _), (pw_r, pw_i) = lax.scan(pw_step, (jnp.ones_like(abar_r), jnp.zeros_like(abar_i)), None, length=lc + 1)
    ca_r = cr[None] * pw_r[:, :, None, :] - ci[None] * pw_i[:, :, None, :]
    ca_i = cr[None] * pw_i[:, :, None, :] + ci[None] * pw_r[:, :, None, :]
    kern = jnp.einsum('tgop,gpc->tgoc', ca_r[:lc], bb_r) - jnp.einsum('tgop,gpc->tgoc', ca_i[:lc], bb_i)
    eye = jnp.eye(c_n, dtype=f32)
    kern = kern.at[0].add(d_skip.astype(f32).reshape(g_n, c_n)[:, :, None] * eye[None])
    s_idx = jnp.arange(lc)[:, None]
    t_idx = jnp.arange(lc)[None, :]
    lag = t_idx - s_idx
    toep = kern[jnp.clip(lag, 0, lc - 1)] * (lag >= 0)[:, :, None, None, None].astype(f32)
    intra = jnp.transpose(toep, (2, 0, 4, 1, 3)).reshape(g_n, lc * c_n, lc * c_n)
    rev_r, rev_i = pw_r[lc - 1 - jnp.arange(lc)], pw_i[lc - 1 - jnp.arange(lc)]
    ts_r, ts_i = _cmul(rev_r[..., None], rev_i[..., None], bb_r[None], bb_i[None])
    to_state = jnp.concatenate([jnp.transpose(ts_r, (1, 0, 3, 2)), jnp.transpose(ts_i, (1, 0, 3, 2))], axis=-1)
    to_state = to_state.reshape(g_n, lc * c_n, 2 * p_n)
    fs_r = jnp.transpose(ca_r[1:lc + 1], (1, 3, 0, 2)).reshape(g_n, p_n, lc * c_n)
    fs_i = jnp.transpose(ca_i[1:lc + 1], (1, 3, 0, 2)).reshape(g_n, p_n, lc * c_n)
    from_state = jnp.concatenate([fs_r, -fs_i], axis=1)
    n_steps = max(1, (n_blocks - 1).bit_length())
    rows = []
    ar, ai = pw_r[lc], pw_i[lc]
    for _ in range(n_steps):
        rows.append(jnp.stack([jnp.concatenate([ar, ar], -1), jnp.concatenate([-ai, ai], -1)], axis=1))
        ar, ai = _cmul(ar, ai, ar, ai)
    apow = jnp.stack(rows, axis=1)
    return intra.astype(BF16), to_state.astype(BF16), from_state.astype(BF16), apow


def _ssm_body(n_blocks, u_ref, intra_ref, ts_ref, fs_ref, apow_ref, o_ref):
    u = u_ref[0]
    y = jnp.dot(u, intra_ref[0], preferred_element_type=F32)
    hs = jnp.dot(u, ts_ref[0], preferred_element_type=F32)
    rows = hs.shape[0]
    half = hs.shape[1] // 2
    blk = lax.broadcasted_iota(jnp.int32, (rows, 1), 0) % n_blocks
    d = 1
    k = 0
    while d < n_blocks:
        sh = jnp.where(blk >= d, pltpu.roll(hs, d, axis=0), 0.0)
        hs = hs + apow_ref[0, k, 0:1, :] * sh + apow_ref[0, k, 1:2, :] * pltpu.roll(sh, half, axis=1)
        d *= 2
        k += 1
    h_in = jnp.where(blk >= 1, pltpu.roll(hs, 1, axis=0), 0.0)
    y = y + jnp.dot(h_in.astype(BF16), fs_ref[0], preferred_element_type=F32)
    o_ref[0] = _gelu(y).astype(BF16)


def _ssm_scan(u_g, intra, to_state, from_state, apow, n_blocks):
    g_n, rows, width = u_g.shape
    p2 = to_state.shape[2]
    n_steps = apow.shape[1]
    return pl.pallas_call(
        functools.partial(_ssm_body, n_blocks),
        grid=(g_n,),
        in_specs=[
            pl.BlockSpec((1, rows, width), lambda g: (g, 0, 0)),
            pl.BlockSpec((1, width, width), lambda g: (g, 0, 0)),
            pl.BlockSpec((1, width, p2), lambda g: (g, 0, 0)),
            pl.BlockSpec((1, p2, width), lambda g: (g, 0, 0)),
            pl.BlockSpec((1, n_steps, 2, p2), lambda g: (g, 0, 0, 0)),
        ],
        out_specs=pl.BlockSpec((1, rows, width), lambda g: (g, 0, 0)),
        out_shape=jax.ShapeDtypeStruct((g_n, rows, width), BF16),
        compiler_params=_params(("arbitrary",)),
        name="ssm_scan",
    )(u_g, intra, to_state, from_state, apow)


def _glu_body(g_ref, w_ref, o_ref):
    g = g_ref[...]
    z = jnp.dot(g, w_ref[...], preferred_element_type=F32)
    o_ref[...] = (g.astype(F32) * jax.nn.sigmoid(z)).astype(BF16)


def _glu(g, w):
    t, d = g.shape
    tm = min(ROW_TILE, t)
    return pl.pallas_call(
        _glu_body,
        grid=(t // tm,),
        in_specs=[pl.BlockSpec((tm, d), lambda i: (i, 0)), pl.BlockSpec((d, d), lambda i: (0, 0))],
        out_specs=pl.BlockSpec((tm, d), lambda i: (i, 0)),
        out_shape=jax.ShapeDtypeStruct((t, d), BF16),
        compiler_params=_params(("arbitrary",)),
        name="ssm_glu",
    )(g, w)


def _ssm_mixer(hb, bsz, seq, w_in, ops, w_glu):
    t, d = hb.shape
    lc = SSM_LC
    n_blocks = seq // lc
    g_n = d // SSM_GROUP_DIM
    u = _matmul(hb, w_in, BF16)
    u_g = u.reshape(bsz * n_blocks, lc, g_n, SSM_GROUP_DIM).transpose(2, 0, 1, 3).reshape(g_n, bsz * n_blocks, lc * SSM_GROUP_DIM)
    y_g = _ssm_scan(u_g, *ops, n_blocks)
    gl = y_g.reshape(g_n, bsz * n_blocks, lc, SSM_GROUP_DIM).transpose(1, 2, 0, 3).reshape(t, d)
    return _glu(gl, w_glu)


def _attn_tables(rel_bias, tq):
    qi = jnp.arange(tq)[:, None]
    kj = jnp.arange(tq)[None, :]
    rb = rel_bias.astype(F32)
    cur = rb[:, jnp.clip(qi - kj, -ATTN_MAX_REL, ATTN_MAX_REL) + ATTN_MAX_REL]
    prev = rb[:, jnp.clip(qi - kj + tq, -ATTN_MAX_REL, ATTN_MAX_REL) + ATTN_MAX_REL]
    cur = jnp.where((kj // CHUNK <= qi // CHUNK)[None], cur, NEG_INF)
    prev = jnp.where((kj // CHUNK >= qi // CHUNK)[None], prev, NEG_INF)
    return prev, cur


def _attn_body(scale, q_ref, kp_ref, kc_ref, vp_ref, vc_ref, bp_ref, bc_ref, o_ref):
    i = pl.program_id(2)
    q = q_ref[...]
    nt = (((1,), (1,)), ((), ()))
    s_p = lax.dot_general(q, kp_ref[...], nt, preferred_element_type=F32) * scale + bp_ref[0]
    s_c = lax.dot_general(q, kc_ref[...], nt, preferred_element_type=F32) * scale + bc_ref[0]
    s_p = jnp.where(i == 0, NEG_INF, s_p)
    m = jnp.maximum(jnp.max(s_p, axis=-1, keepdims=True), jnp.max(s_c, axis=-1, keepdims=True))
    p_p = jnp.exp(s_p - m)
    p_c = jnp.exp(s_c - m)
    l = jnp.sum(p_p, axis=-1, keepdims=True) + jnp.sum(p_c, axis=-1, keepdims=True)
    o = jnp.dot(p_p.astype(BF16), vp_ref[...], preferred_element_type=F32)
    o = o + jnp.dot(p_c.astype(BF16), vc_ref[...], preferred_element_type=F32)
    o_ref[...] = (o / l).astype(BF16)


def _chunk_attention(qkv, bsz, seq, heads, bias_prev, bias_cur):
    t = qkv.shape[0]
    hd = qkv.shape[1] // (3 * heads)
    tq = ATTN_TQ
    nt = seq // tq
    cur = lambda h, b, i: (b * nt + i, 0)
    prev = lambda h, b, i: (b * nt + jnp.maximum(i - 1, 0), 0)

    def col(base, rows):
        return lambda h, b, i: (rows(h, b, i)[0], base + h)

    return pl.pallas_call(
        functools.partial(_attn_body, hd ** -0.5),
        grid=(heads, bsz, nt),
        in_specs=[
            pl.BlockSpec((tq, hd), col(0, cur)),
            pl.BlockSpec((tq, hd), col(heads, prev)),
            pl.BlockSpec((tq, hd), col(heads, cur)),
            pl.BlockSpec((tq, hd), col(2 * heads, prev)),
            pl.BlockSpec((tq, hd), col(2 * heads, cur)),
            pl.BlockSpec((1, tq, tq), lambda h, b, i: (h, 0, 0)),
            pl.BlockSpec((1, tq, tq), lambda h, b, i: (h, 0, 0)),
        ],
        out_specs=pl.BlockSpec((tq, hd), lambda h, b, i: (b * nt + i, h)),
        out_shape=jax.ShapeDtypeStruct((t, heads * hd), BF16),
        compiler_params=_params(("arbitrary", "arbitrary", "arbitrary")),
        name="chunk_attention",
    )(qkv, qkv, qkv, qkv, qkv, bias_prev, bias_cur)


def _top_rows(src, n, dst_ref):
    cur = src
    for r in range(n):
        m = jnp.max(cur, axis=0, keepdims=True)
        dst_ref[r:r + 1, :] = m
        if r + 1 < n:
            cur = jnp.where(cur == m, -jnp.inf, cur)


def _peer_score_body(ht_ref, wqt_ref, keys_ref, s1_ref, s2_ref, st_ref, q_scr, t1_scr, t2_scr, cand_scr, best_scr):
    k = PEER_TOPK
    half = PEER_HALF
    q_scr[...] = jnp.dot(wqt_ref[...], ht_ref[...], preferred_element_type=F32).astype(BF16)

    def head(h, carry):
        base = pl.multiple_of(h * 2 * half, 2 * half)
        s1 = jnp.dot(keys_ref[0], q_scr[pl.ds(base, half), :], preferred_element_type=F32)
        s2 = jnp.dot(keys_ref[1], q_scr[pl.ds(base + half, half), :], preferred_element_type=F32)
        s1_ref[h] = s1
        s2_ref[h] = s2
        _top_rows(s1, k + 1, t1_scr)
        _top_rows(s2, k + 1, t2_scr)
        b16 = t2_scr[0:k, :]
        for a in range(k):
            cand_scr[a * k:(a + 1) * k, :] = t1_scr[a:a + 1, :] + b16
        cur = cand_scr[...]
        for r in range(k):
            m = jnp.max(cur, axis=0, keepdims=True)
            best_scr[r:r + 1, :] = m
            cur = jnp.where(cur == m, -jnp.inf, cur)
        nxt = jnp.max(cur, axis=0, keepdims=True)
        nxt = jnp.maximum(nxt, t1_scr[k:k + 1, :] + t2_scr[0:1, :])
        nxt = jnp.maximum(nxt, t1_scr[0:1, :] + t2_scr[k:k + 1, :])
        best = best_scr[...]
        z = jnp.sum(jnp.exp(best - best[0:1, :]), axis=0, keepdims=True)
        st_ref[0, pl.ds(h, 1), :] = t1_scr[0:1, :]
        st_ref[1, pl.ds(h, 1), :] = t2_scr[0:1, :]
        st_ref[2, pl.ds(h, 1), :] = z
        st_ref[3, pl.ds(h, 1), :] = 0.5 * (best[k - 1:k, :] + nxt)
        return carry

    lax.fori_loop(0, PEER_HEADS, head, 0)


def _peer_scores(ht, wqt, keys):
    d, t = ht.shape
    qd = wqt.shape[0]
    ts = min(PEER_TS, t)
    nk = keys.shape[1]
    k = PEER_TOPK
    return pl.pallas_call(
        _peer_score_body,
        grid=(t // ts,),
        in_specs=[
            pl.BlockSpec((d, ts), lambda i: (0, i)),
            pl.BlockSpec((qd, d), lambda i: (0, 0)),
            pl.BlockSpec((2, nk, PEER_HALF), lambda i: (0, 0, 0)),
        ],
        out_specs=[
            pl.BlockSpec((PEER_HEADS, nk, ts), lambda i: (0, 0, i)),
            pl.BlockSpec((PEER_HEADS, nk, ts), lambda i: (0, 0, i)),
            pl.BlockSpec((4, PEER_HEADS, ts), lambda i: (0, 0, i)),
        ],
        out_shape=[
            jax.ShapeDtypeStruct((PEER_HEADS, nk, t), F32),
            jax.ShapeDtypeStruct((PEER_HEADS, nk, t), F32),
            jax.ShapeDtypeStruct((4, PEER_HEADS, t), F32),
        ],
        scratch_shapes=[
            pltpu.VMEM((qd, ts), BF16),
            pltpu.VMEM((24, ts), F32),
            pltpu.VMEM((24, ts), F32),
            pltpu.VMEM((k * k, ts), F32),
            pltpu.VMEM((k, ts), F32),
        ],
        compiler_params=_params(("arbitrary",)),
        name="peer_scores",
    )(ht, wqt, keys)


def _peer_dense_body(alpha, ht_ref, u_ref, vt_ref, s1_ref, s2_ref, st_ref, res_ref, g_ref, b_ref,
                     h_ref, hb_ref, acc_ref, e2_ref, f1_ref, thr_ref):
    j = pl.program_id(1)
    nk = s1_ref.shape[1]
    blocks = u_ref.shape[0] // nk

    @pl.when(j == 0)
    def _():
        for h in range(PEER_HEADS):
            m1 = st_ref[0, h:h + 1, :]
            m2 = st_ref[1, h:h + 1, :]
            z = st_ref[2, h:h + 1, :]
            tau = st_ref[3, h:h + 1, :]
            s1 = s1_ref[h]
            e2_ref[h] = jnp.exp(s2_ref[h] - m2)
            f1_ref[h] = jnp.exp(s1 - m1) / z
            thr_ref[h] = tau - s1
        acc_ref[...] = jnp.zeros(acc_ref.shape, F32)

    st = jnp.dot(u_ref[...], ht_ref[...], preferred_element_type=F32)
    parts = []
    for jj in range(blocks):
        row = j * blocks + jj
        wg = jnp.zeros((nk, st.shape[1]), F32)
        for h in range(PEER_HEADS):
            thr = thr_ref[h, pl.ds(row, 1), :]
            f1 = f1_ref[h, pl.ds(row, 1), :]
            wg = wg + jnp.where(s2_ref[h] >= thr, e2_ref[h], 0.0) * f1
        parts.append((wg * _gelu(st[jj * nk:(jj + 1) * nk, :])).astype(BF16))
    wp = jnp.concatenate(parts, axis=0) if blocks > 1 else parts[0]
    acc_ref[...] += jnp.dot(vt_ref[...], wp, preferred_element_type=F32)

    @pl.when(j == pl.num_programs(1) - 1)
    def _():
        y = _layer_norm(alpha * res_ref[...] + acc_ref[...].T, g_ref[...], b_ref[...])
        h_ref[...] = y
        hb_ref[...] = y.astype(BF16)


def _peer_dense(ht, u, vt, s1, s2, stats, res, g, b, alpha):
    d, t = ht.shape
    ne = u.shape[0]
    nk = s1.shape[1]
    tt = min(PEER_TT, t)
    eb = PEER_EB
    hds = PEER_HEADS
    return pl.pallas_call(
        functools.partial(_peer_dense_body, alpha),
        grid=(t // tt, ne // eb),
        in_specs=[
            pl.BlockSpec((d, tt), lambda i, j: (0, i)),
            pl.BlockSpec((eb, d), lambda i, j: (j, 0)),
            pl.BlockSpec((d, eb), lambda i, j: (0, j)),
            pl.BlockSpec((hds, nk, tt), lambda i, j: (0, 0, i)),
            pl.BlockSpec((hds, nk, tt), lambda i, j: (0, 0, i)),
            pl.BlockSpec((4, hds, tt), lambda i, j: (0, 0, i)),
            pl.BlockSpec((tt, d), lambda i, j: (i, 0)),
            pl.BlockSpec((1, d), lambda i, j: (0, 0)),
            pl.BlockSpec((1, d), lambda i, j: (0, 0)),
        ],
        out_specs=[pl.BlockSpec((tt, d), lambda i, j: (i, 0)), pl.BlockSpec((tt, d), lambda i, j: (i, 0))],
        out_shape=[jax.ShapeDtypeStruct((t, d), F32), jax.ShapeDtypeStruct((t, d), BF16)],
        scratch_shapes=[
            pltpu.VMEM((d, tt), F32),
            pltpu.VMEM((hds, nk, tt), F32),
            pltpu.VMEM((hds, nk, tt), F32),
            pltpu.VMEM((hds, nk, tt), F32),
        ],
        compiler_params=_params(("arbitrary", "arbitrary")),
        name="peer_dense",
    )(ht, u, vt, s1, s2, stats, res, g.reshape(1, d), b.reshape(1, d))


def kernel(x, p, ln_mix_g, ln_mix_b, ln_ffn_g, ln_ffn_b, pool_w_in, pool_w_group, pool_scale, pool_w_out, ssm_w_in, ssm_lambda_re, ssm_lambda_im, ssm_log_step, ssm_b_re, ssm_b_im, ssm_c_re, ssm_c_im, ssm_d, ssm_w_glu, ssm_w_out, attn_w_qkv, attn_rel_bias, attn_w_out, peer_w_q, peer_sub_keys, peer_u, peer_v, ple_w_proj, ple_w_gate, ple_b_gate):
    bsz, seq, d = x.shape
    depth = p.shape[0]
    t = bsz * seq
    alpha = (2.0 * depth) ** 0.25
    h = x.reshape(t, d)
    hb = h.astype(BF16)
    for i in range(depth):
        kind, j = i % N_MIXERS, i // N_MIXERS
        if kind == 0:
            mixed = _pool_mixer(hb, seq, pool_w_in[j].astype(BF16), pool_w_group[j].astype(BF16), pool_scale[j])
            w_out = pool_w_out[j]
        elif kind == 1:
            ops = _ssm_operators(ssm_lambda_re[j], ssm_lambda_im[j], ssm_log_step[j], ssm_b_re[j], ssm_b_im[j],
                                 ssm_c_re[j], ssm_c_im[j], ssm_d[j], SSM_LC, seq // SSM_LC)
            mixed = _ssm_mixer(hb, bsz, seq, ssm_w_in[j].astype(BF16), ops, ssm_w_glu[j].astype(BF16))
            w_out = ssm_w_out[j]
        else:
            qkv = _matmul(hb, attn_w_qkv[j].astype(BF16), BF16)
            bias_prev, bias_cur = _attn_tables(attn_rel_bias[j], ATTN_TQ)
            mixed = _chunk_attention(qkv, bsz, seq, ATTN_HEADS, bias_prev, bias_cur)
            w_out = attn_w_out[j]
        h, ht = _matmul_ln(mixed, w_out.astype(BF16), h, ln_mix_g[i], ln_mix_b[i], alpha)
        s1, s2, stats = _peer_scores(ht, peer_w_q[i].T.astype(BF16), peer_sub_keys[i].astype(BF16))
        h, hb = _peer_dense(ht, peer_u[i].astype(BF16), peer_v[i].T.astype(BF16), s1, s2, stats,
                            h, ln_ffn_g[i], ln_ffn_b[i], alpha)
        h, hb = _ple(hb, h, p[i].reshape(t, -1).astype(BF16), ple_w_gate[i].astype(BF16),
                     ple_w_proj[i].astype(BF16), ple_b_gate[i])
    return h.reshape(bsz, seq, d)
```

```python
import functools
import math

import jax
import jax.numpy as jnp
import numpy as np
from jax import lax
from jax.experimental import pallas as pl
from jax.experimental.pallas import tpu as pltpu

F32 = jnp.float32
BF16 = jnp.bfloat16

CHUNK = 64
N_MIXERS = 3
LN_EPS = 1e-5
POOL_WINDOWS = (2, 4, 8, 16)
POOL_MAX_WINDOW = max(POOL_WINDOWS)
SSM_GROUP_DIM = 16
SSM_STATE = 64
ATTN_HEADS = 16
ATTN_LEFT_CHUNKS = 8
ATTN_MAX_REL = 2 * CHUNK
NEG_INF = -1e30
PEER_N_KEYS = 128
PEER_HEADS = 8
PEER_HALF = 128
PEER_TOPK = 16

LANES = 128
VMEM_LIMIT = 56 * 1024 * 1024
ROW_TILE = 256
MM_TM, MM_TN = 512, 1024
ATTN_TQ = ATTN_LEFT_CHUNKS * CHUNK
PEER_TS = 8 * LANES
PEER_TT = 512
PEER_EB = 512
PEER_RC = 16
SSM_LC = 64


def _params(sem):
    return pltpu.CompilerParams(dimension_semantics=sem, vmem_limit_bytes=VMEM_LIMIT)


def _divisor_tile(n, preferred):
    best = LANES
    for cand in range(LANES, min(n, preferred) + 1, LANES):
        if n % cand == 0:
            best = cand
    assert n % best == 0, (n, preferred)
    return best


def _gelu(x):
    return 0.5 * x * (1.0 + lax.erf(x * (1.0 / math.sqrt(2.0))))


def _layer_norm(z, g, b):
    mu = jnp.mean(z, axis=-1, keepdims=True)
    zc = z - mu
    var = jnp.mean(zc * zc, axis=-1, keepdims=True)
    return zc * lax.rsqrt(var + LN_EPS) * g + b


def _mm_body(a_ref, w_ref, o_ref):
    o_ref[...] = jnp.dot(a_ref[...], w_ref[...], preferred_element_type=F32).astype(o_ref.dtype)


def _matmul(a, w, out_dtype):
    m, k = a.shape
    n = w.shape[1]
    tm, tn = _divisor_tile(m, MM_TM), _divisor_tile(n, MM_TN)
    return pl.pallas_call(
        _mm_body,
        grid=(n // tn, m // tm),
        in_specs=[pl.BlockSpec((tm, k), lambda j, i: (i, 0)), pl.BlockSpec((k, tn), lambda j, i: (0, j))],
        out_specs=pl.BlockSpec((tm, tn), lambda j, i: (i, j)),
        out_shape=jax.ShapeDtypeStruct((m, n), out_dtype),
        compiler_params=_params(("arbitrary", "arbitrary")),
        name="matmul",
    )(a, w)


def _mm_ln_body(alpha, a_ref, w_ref, res_ref, g_ref, b_ref, h_ref, hb_ref, ht_ref):
    m = jnp.dot(a_ref[...], w_ref[...], preferred_element_type=F32)
    y = _layer_norm(alpha * res_ref[...] + m, g_ref[...], b_ref[...])
    h_ref[...] = y
    hb_ref[...] = y.astype(BF16)
    ht_ref[...] = y.T.astype(BF16)


def _matmul_ln(a, w, res, g, b, alpha):
    t, k = a.shape
    d = w.shape[1]
    tm = min(ROW_TILE, t)
    return pl.pallas_call(
        functools.partial(_mm_ln_body, alpha),
        grid=(t // tm,),
        in_specs=[
            pl.BlockSpec((tm, k), lambda i: (i, 0)),
            pl.BlockSpec((k, d), lambda i: (0, 0)),
            pl.BlockSpec((tm, d), lambda i: (i, 0)),
            pl.BlockSpec((1, d), lambda i: (0, 0)),
            pl.BlockSpec((1, d), lambda i: (0, 0)),
        ],
        out_specs=[pl.BlockSpec((tm, d), lambda i: (i, 0)), pl.BlockSpec((tm, d), lambda i: (i, 0)),
                   pl.BlockSpec((d, tm), lambda i: (0, i))],
        out_shape=[jax.ShapeDtypeStruct((t, d), F32), jax.ShapeDtypeStruct((t, d), BF16),
                   jax.ShapeDtypeStruct((d, t), BF16)],
        compiler_params=_params(("arbitrary",)),
        name="outproj_ln",
    )(a, w, res, g.reshape(1, d), b.reshape(1, d))


def _ple_body(hb_ref, h_ref, p_ref, wg_ref, wp_ref, bg_ref, o_ref, ob_ref):
    gate = jax.nn.sigmoid(jnp.dot(hb_ref[...], wg_ref[...], preferred_element_type=F32) + bg_ref[...])
    proj = jnp.dot(p_ref[...], wp_ref[...], preferred_element_type=F32)
    y = h_ref[...] + gate * proj
    o_ref[...] = y
    ob_ref[...] = y.astype(BF16)


def _ple(hb, h, p, wg, wp, bg):
    t, d = h.shape
    pd = p.shape[1]
    tm = min(ROW_TILE, t)
    return pl.pallas_call(
        _ple_body,
        grid=(t // tm,),
        in_specs=[
            pl.BlockSpec((tm, d), lambda i: (i, 0)),
            pl.BlockSpec((tm, d), lambda i: (i, 0)),
            pl.BlockSpec((tm, pd), lambda i: (i, 0)),
            pl.BlockSpec((d, d), lambda i: (0, 0)),
            pl.BlockSpec((pd, d), lambda i: (0, 0)),
            pl.BlockSpec((1, d), lambda i: (0, 0)),
        ],
        out_specs=[pl.BlockSpec((tm, d), lambda i: (i, 0)), pl.BlockSpec((tm, d), lambda i: (i, 0))],
        out_shape=[jax.ShapeDtypeStruct((t, d), F32), jax.ShapeDtypeStruct((t, d), BF16)],
        compiler_params=_params(("arbitrary",)),
        name="ple_gate",
    )(hb, h, p, wg, wp, bg.reshape(1, d))


def _pool_body(tiles_per_seq, x_ref, win_ref, wg_ref, sc_ref, o_ref, ext_ref):
    wmax = POOL_MAX_WINDOW
    tm = x_ref.shape[0]
    gd = wg_ref.shape[1]
    i = pl.program_id(0)

    @pl.when(i % tiles_per_seq == 0)
    def _():
        ext_ref[0:wmax, :] = jnp.zeros((wmax, ext_ref.shape[1]), F32)

    ext_ref[wmax:wmax + tm, :] = jnp.dot(x_ref[...], win_ref[...], preferred_element_type=F32)
    pos = (i % tiles_per_seq) * tm + lax.broadcasted_iota(jnp.int32, (tm, 1), 0)
    for g, w in enumerate(POOL_WINDOWS):
        cols = slice(g * gd, (g + 1) * gd)
        u = ext_ref[wmax:wmax + tm, cols]
        acc = u
        for k in range(1, w):
            acc = acc + ext_ref[wmax - k:wmax - k + tm, cols]
        count = jnp.minimum(pos + 1, w).astype(F32)
        pooled = acc / count - u
        mixed = jnp.dot(pooled.astype(BF16), wg_ref[g], preferred_element_type=F32) * sc_ref[:, cols]
        o_ref[:, cols] = mixed.astype(BF16)
    ext_ref[0:wmax, :] = ext_ref[tm:tm + wmax, :]


def _pool_mixer(hb, seq, w_in, w_group, scale):
    t, d = hb.shape
    tm = min(ROW_TILE, seq)
    ng, gd = w_group.shape[0], w_group.shape[1]
    return pl.pallas_call(
        functools.partial(_pool_body, seq // tm),
        grid=(t // tm,),
        in_specs=[
            pl.BlockSpec((tm, d), lambda i: (i, 0)),
            pl.BlockSpec((d, d), lambda i: (0, 0)),
            pl.BlockSpec((ng, gd, gd), lambda i: (0, 0, 0)),
            pl.BlockSpec((1, d), lambda i: (0, 0)),
        ],
        out_specs=pl.BlockSpec((tm, d), lambda i: (i, 0)),
        out_shape=jax.ShapeDtypeStruct((t, d), BF16),
        scratch_shapes=[pltpu.VMEM((tm + POOL_MAX_WINDOW, d), F32)],
        compiler_params=_params(("arbitrary",)),
        name="pool_mixer",
    )(hb, w_in, w_group, scale.reshape(1, d))


def _cmul(ar, ai, br, bi):
    return ar * br - ai * bi, ar * bi + ai * br


def _ssm_operators(lam_re, lam_im, log_step, b_re, b_im, c_re, c_im, d_skip, lc, n_blocks):
    f32 = F32
    lr, li = lam_re.astype(f32), lam_im.astype(f32)
    dt = jnp.exp(log_step.astype(f32))[:, None]
    mag = jnp.exp(dt * lr)
    abar_r, abar_i = mag * jnp.cos(dt * li), mag * jnp.sin(dt * li)
    den = lr * lr + li * li
    k_r = ((abar_r - 1.0) * lr + abar_i * li) / den
    k_i = (abar_i * lr - (abar_r - 1.0) * li) / den
    bb_r, bb_i = _cmul(k_r[..., None], k_i[..., None], b_re.astype(f32), b_im.astype(f32))
    cr, ci = c_re.astype(f32), c_im.astype(f32)
    g_n, p_n = lr.shape
    c_n = bb_r.shape[-1]

    def pw_step(carry, _):
        pr, pi = carry
        nr, ni = _cmul(pr, pi, abar_r, abar_i)
        return (nr, ni), (pr, pi)

    (_, _), (pw_r, pw_i) = lax.scan(pw_step, (jnp.ones_like(abar_r), jnp.zeros_like(abar_i)), None, length=lc + 1)
    ca_r = cr[None] * pw_r[:, :, None, :] - ci[None] * pw_i[:, :, None, :]
    ca_i = cr[None] * pw_i[:, :, None, :] + ci[None] * pw_r[:, :, None, :]
    kern = jnp.einsum('tgop,gpc->tgoc', ca_r[:lc], bb_r) - jnp.einsum('tgop,gpc->tgoc', ca_i[:lc], bb_i)
    eye = jnp.eye(c_n, dtype=f32)
    kern = kern.at[0].add(d_skip.astype(f32).reshape(g_n, c_n)[:, :, None] * eye[None])
    s_idx = jnp.arange(lc)[:, None]
    t_idx = jnp.arange(lc)[None, :]
    lag = t_idx - s_idx
    toep = kern[jnp.clip(lag, 0, lc - 1)] * (lag >= 0)[:, :, None, None, None].astype(f32)
    intra = jnp.transpose(toep, (2, 0, 4, 1, 3)).reshape(g_n, lc * c_n, lc * c_n)
    rev_r, rev_i = pw_r[lc - 1 - jnp.arange(lc)], pw_i[lc - 1 - jnp.arange(lc)]
    ts_r, ts_i = _cmul(rev_r[..., None], rev_i[..., None], bb_r[None], bb_i[None])
    to_state = jnp.concatenate([jnp.transpose(ts_r, (1, 0, 3, 2)), jnp.transpose(ts_i, (1, 0, 3, 2))], axis=-1)
    to_state = to_state.reshape(g_n, lc * c_n, 2 * p_n)
    fs_r = jnp.transpose(ca_r[1:lc + 1], (1, 3, 0, 2)).reshape(g_n, p_n, lc * c_n)
    fs_i = jnp.transpose(ca_i[1:lc + 1], (1, 3, 0, 2)).reshape(g_n, p_n, lc * c_n)
    from_state = jnp.concatenate([fs_r, -fs_i], axis=1)
    n_steps = max(1, (n_blocks - 1).bit_length())
    rows = []
    ar, ai = pw_r[lc], pw_i[lc]
    for _ in range(n_steps):
        rows.append(jnp.stack([jnp.concatenate([ar, ar], -1), jnp.concatenate([-ai, ai], -1)], axis=1))
        ar, ai = _cmul(ar, ai, ar, ai)
    apow = jnp.stack(rows, axis=1)
    return intra.astype(BF16), to_state.astype(BF16), from_state.astype(BF16), apow


def _ssm_body(n_blocks, u_ref, intra_ref, ts_ref, fs_ref, apow_ref, o_ref):
    u = u_ref[0]
    y = jnp.dot(u, intra_ref[0], preferred_element_type=F32)
    hs = jnp.dot(u, ts_ref[0], preferred_element_type=F32)
    rows = hs.shape[0]
    half = hs.shape[1] // 2
    blk = lax.broadcasted_iota(jnp.int32, (rows, 1), 0) % n_blocks
    d = 1
    k = 0
    while d < n_blocks:
        sh = jnp.where(blk >= d, pltpu.roll(hs, d, axis=0), 0.0)
        hs = hs + apow_ref[0, k, 0:1, :] * sh + apow_ref[0, k, 1:2, :] * pltpu.roll(sh, half, axis=1)
        d *= 2
        k += 1
    h_in = jnp.where(blk >= 1, pltpu.roll(hs, 1, axis=0), 0.0)
    y = y + jnp.dot(h_in.astype(BF16), fs_ref[0], preferred_element_type=F32)
    o_ref[0] = _gelu(y).astype(BF16)


def _ssm_scan(u_g, intra, to_state, from_state, apow, n_blocks):
    g_n, rows, width = u_g.shape
    p2 = to_state.shape[2]
    n_steps = apow.shape[1]
    return pl.pallas_call(
        functools.partial(_ssm_body, n_blocks),
        grid=(g_n,),
        in_specs=[
            pl.BlockSpec((1, rows, width), lambda g: (g, 0, 0)),
            pl.BlockSpec((1, width, width), lambda g: (g, 0, 0)),
            pl.BlockSpec((1, width, p2), lambda g: (g, 0, 0)),
            pl.BlockSpec((1, p2, width), lambda g: (g, 0, 0)),
            pl.BlockSpec((1, n_steps, 2, p2), lambda g: (g, 0, 0, 0)),
        ],
        out_specs=pl.BlockSpec((1, rows, width), lambda g: (g, 0, 0)),
        out_shape=jax.ShapeDtypeStruct((g_n, rows, width), BF16),
        compiler_params=_params(("arbitrary",)),
        name="ssm_scan",
    )(u_g, intra, to_state, from_state, apow)


def _glu_body(g_ref, w_ref, o_ref):
    g = g_ref[...]
    z = jnp.dot(g, w_ref[...], preferred_element_type=F32)
    o_ref[...] = (g.astype(F32) * jax.nn.sigmoid(z)).astype(BF16)


def _glu(g, w):
    t, d = g.shape
    tm = min(ROW_TILE, t)
    return pl.pallas_call(
        _glu_body,
        grid=(t // tm,),
        in_specs=[pl.BlockSpec((tm, d), lambda i: (i, 0)), pl.BlockSpec((d, d), lambda i: (0, 0))],
        out_specs=pl.BlockSpec((tm, d), lambda i: (i, 0)),
        out_shape=jax.ShapeDtypeStruct((t, d), BF16),
        compiler_params=_params(("arbitrary",)),
        name="ssm_glu",
    )(g, w)


def _ssm_mixer(hb, bsz, seq, w_in, ops, w_glu):
    t, d = hb.shape
    lc = SSM_LC
    n_blocks = seq // lc
    g_n = d // SSM_GROUP_DIM
    u = _matmul(hb, w_in, BF16)
    u_g = u.reshape(bsz * n_blocks, lc, g_n, SSM_GROUP_DIM).transpose(2, 0, 1, 3).reshape(g_n, bsz * n_blocks, lc * SSM_GROUP_DIM)
    y_g = _ssm_scan(u_g, *ops, n_blocks)
    gl = y_g.reshape(g_n, bsz * n_blocks, lc, SSM_GROUP_DIM).transpose(1, 2, 0, 3).reshape(t, d)
    return _glu(gl, w_glu)


def _attn_tables(rel_bias, tq):
    rb = rel_bias.astype(F32)
    heads = rb.shape[0]
    period = 2 * tq - 1

    def toeplitz(offset):
        m = np.arange(period)
        diff = np.where(m < tq, -m, period - m) + offset
        diag = rb[:, np.clip(diff, -ATTN_MAX_REL, ATTN_MAX_REL) + ATTN_MAX_REL]
        rows = jnp.tile(diag, (1, tq))[:, :tq * (period - 1)].reshape(heads, tq, period - 1)
        return rows[:, :, :tq]

    qc = np.arange(tq)[:, None] // CHUNK
    kc = np.arange(tq)[None, :] // CHUNK
    cur = jnp.where((kc <= qc)[None], toeplitz(0), NEG_INF)
    prev = jnp.where((kc >= qc)[None], toeplitz(tq), NEG_INF)
    return prev, cur


def _attn_body(scale, q_ref, kp_ref, kc_ref, vp_ref, vc_ref, bp_ref, bc_ref, o_ref):
    i = pl.program_id(2)
    q = q_ref[...]
    nt = (((1,), (1,)), ((), ()))
    s_p = lax.dot_general(q, kp_ref[...], nt, preferred_element_type=F32) * scale + bp_ref[0]
    s_c = lax.dot_general(q, kc_ref[...], nt, preferred_element_type=F32) * scale + bc_ref[0]
    s_p = jnp.where(i == 0, NEG_INF, s_p)
    m = jnp.maximum(jnp.max(s_p, axis=-1, keepdims=True), jnp.max(s_c, axis=-1, keepdims=True))
    p_p = jnp.exp(s_p - m)
    p_c = jnp.exp(s_c - m)
    l = jnp.sum(p_p, axis=-1, keepdims=True) + jnp.sum(p_c, axis=-1, keepdims=True)
    o = jnp.dot(p_p.astype(BF16), vp_ref[...], preferred_element_type=F32)
    o = o + jnp.dot(p_c.astype(BF16), vc_ref[...], preferred_element_type=F32)
    o_ref[...] = (o / l).astype(BF16)


def _chunk_attention(qkv, bsz, seq, heads, bias_prev, bias_cur):
    t = qkv.shape[0]
    hd = qkv.shape[1] // (3 * heads)
    tq = ATTN_TQ
    nt = seq // tq
    cur = lambda h, b, i: (b * nt + i, 0)
    prev = lambda h, b, i: (b * nt + jnp.maximum(i - 1, 0), 0)

    def col(base, rows):
        return lambda h, b, i: (rows(h, b, i)[0], base + h)

    return pl.pallas_call(
        functools.partial(_attn_body, hd ** -0.5),
        grid=(heads, bsz, nt),
        in_specs=[
            pl.BlockSpec((tq, hd), col(0, cur)),
            pl.BlockSpec((tq, hd), col(heads, prev)),
            pl.BlockSpec((tq, hd), col(heads, cur)),
            pl.BlockSpec((tq, hd), col(2 * heads, prev)),
            pl.BlockSpec((tq, hd), col(2 * heads, cur)),
            pl.BlockSpec((1, tq, tq), lambda h, b, i: (h, 0, 0)),
            pl.BlockSpec((1, tq, tq), lambda h, b, i: (h, 0, 0)),
        ],
        out_specs=pl.BlockSpec((tq, hd), lambda h, b, i: (b * nt + i, h)),
        out_shape=jax.ShapeDtypeStruct((t, heads * hd), BF16),
        compiler_params=_params(("arbitrary", "arbitrary", "arbitrary")),
        name="chunk_attention",
    )(qkv, qkv, qkv, qkv, qkv, bias_prev, bias_cur)


def _sort_pairs(n):
    pairs = []
    p = 1
    while p < n:
        k = p
        while k >= 1:
            for j in range(k % p, n - k, 2 * k):
                for i in range(min(k, n - j - k)):
                    if (i + j) // (2 * p) == (i + j + k) // (2 * p):
                        pairs.append((i + j, i + j + k))
            k //= 2
        p *= 2
    return pairs


def _cmp_exchange(x, y):
    if x is None:
        return y, None
    if y is None:
        return x, None
    return jnp.maximum(x, y), jnp.minimum(x, y)


def _opt_max(x, y):
    if x is None:
        return y
    if y is None:
        return x
    return jnp.maximum(x, y)


def _merge_top(a, a_next, b, b_next):
    k = len(a)
    hi, lo_max = [], None
    for i in range(k):
        h, l = _cmp_exchange(a[i], b[k - 1 - i])
        hi.append(h)
        lo_max = _opt_max(lo_max, l)
    d = k // 2
    while d >= 1:
        for i in range(k):
            if i & d == 0:
                hi[i], hi[i + d] = _cmp_exchange(hi[i], hi[i + d])
        d //= 2
    return hi, _opt_max(_opt_max(lo_max, a_next), b_next)


def _top_sorted(vals, k):
    vals = list(vals) + [None] * (-len(vals) % k)
    pairs = _sort_pairs(k)
    groups = []
    for g in range(0, len(vals), k):
        v = vals[g:g + k]
        for i, j in pairs:
            v[i], v[j] = _cmp_exchange(v[i], v[j])
        groups.append((v, None))
    while len(groups) > 1:
        merged = [_merge_top(*groups[i], *groups[i + 1]) for i in range(0, len(groups) - 1, 2)]
        groups = merged + groups[len(groups) - len(groups) % 2:]
    return groups[0]


def _peer_score_body(q_ref, keys_ref, e2_ref, f1_ref, thr_ref, km1_scr, km2_scr, stat_scr):
    k = PEER_TOPK
    nk = keys_ref.shape[1]
    half = keys_ref.shape[2]
    ts = q_ref.shape[1]
    groups = ts // LANES
    assert groups == 8, "one sublane per 128-token group"

    def head(h, carry):
        base = pl.multiple_of(h * 2 * half, 2 * half)
        s1 = jnp.dot(keys_ref[0], q_ref[pl.ds(base, half), :], preferred_element_type=F32)
        s2 = jnp.dot(keys_ref[1], q_ref[pl.ds(base + half, half), :], preferred_element_type=F32)
        for g in range(groups):
            km1_scr[g * nk:(g + 1) * nk, :] = s1[:, g * LANES:(g + 1) * LANES]
            km2_scr[g * nk:(g + 1) * nk, :] = s2[:, g * LANES:(g + 1) * LANES]
        v1 = [km1_scr[pl.ds(key, groups, stride=nk), :] for key in range(nk)]
        v2 = [km2_scr[pl.ds(key, groups, stride=nk), :] for key in range(nk)]
        a, a_next = _top_sorted(v1, k)
        b, b_next = _top_sorted(v2, k)
        cand = [a[i] + b[j] for i in range(k) for j in range(k) if (i + 1) * (j + 1) <= k]
        cand += [a_next + b[0], a[0] + b_next]
        best, nxt = _top_sorted(cand, k)
        z = 1.0 + jnp.exp(best[1] - best[0])
        for r in range(2, k):
            z = z + jnp.exp(best[r] - best[0])
        stat_scr[0 * groups:1 * groups, :] = a[0]
        stat_scr[1 * groups:2 * groups, :] = b[0]
        stat_scr[2 * groups:3 * groups, :] = z
        stat_scr[3 * groups:4 * groups, :] = 0.5 * (best[k - 1] + nxt)
        for g in range(groups):
            cols = slice(g * LANES, (g + 1) * LANES)
            m1 = stat_scr[0 * groups + g:0 * groups + g + 1, :]
            m2 = stat_scr[1 * groups + g:1 * groups + g + 1, :]
            zg = stat_scr[2 * groups + g:2 * groups + g + 1, :]
            tau = stat_scr[3 * groups + g:3 * groups + g + 1, :]
            s1g = km1_scr[g * nk:(g + 1) * nk, :]
            s2g = km2_scr[g * nk:(g + 1) * nk, :]
            e2_ref[h, :, cols] = jnp.exp(s2g - m2)
            f1_ref[h, :, cols] = jnp.exp(s1g - m1) / zg
            thr_ref[h, :, cols] = jnp.exp(tau - s1g - m2)
        return carry

    lax.fori_loop(0, PEER_HEADS, head, 0)


def _peer_scores(qt, keys):
    qd, t = qt.shape
    ts = PEER_TS
    nk = keys.shape[1]
    groups = ts // LANES
    spec = pl.BlockSpec((PEER_HEADS, nk, ts), lambda i: (0, 0, i))
    shape = jax.ShapeDtypeStruct((PEER_HEADS, nk, t), F32)
    return pl.pallas_call(
        _peer_score_body,
        grid=(t // ts,),
        in_specs=[
            pl.BlockSpec((qd, ts), lambda i: (0, i)),
            pl.BlockSpec((2, nk, PEER_HALF), lambda i: (0, 0, 0)),
        ],
        out_specs=[spec, spec, spec],
        out_shape=[shape, shape, shape],
        scratch_shapes=[
            pltpu.VMEM((groups * nk, LANES), F32),
            pltpu.VMEM((groups * nk, LANES), F32),
            pltpu.VMEM((4 * groups, LANES), F32),
        ],
        compiler_params=_params(("arbitrary",)),
        name="peer_scores",
    )(qt, keys)


def _peer_stage_rows(thr_ref, f1_ref, rows_ref, base, blk, n_rows):
    last_row = thr_ref.shape[1] - 1
    for jj in range(n_rows):
        row = jnp.clip(blk * n_rows + jj, 0, last_row)
        for h in range(PEER_HEADS):
            at = base + (jj * 2) * PEER_HEADS + h
            rows_ref[at:at + 1, :] = thr_ref[h, pl.ds(row, 1), :]
            rows_ref[at + PEER_HEADS:at + PEER_HEADS + 1, :] = f1_ref[h, pl.ds(row, 1), :]


def _peer_gate_piece(e2_ref, rows_ref, base, st_ref, wp_ref, jj, ls):
    nk = e2_ref.shape[1]
    rc_n = PEER_RC
    at = base + (jj * 2) * PEER_HEADS
    thr = [jnp.broadcast_to(rows_ref[at + h:at + h + 1, ls], (rc_n, LANES)) for h in range(PEER_HEADS)]
    f1 = [jnp.broadcast_to(rows_ref[at + PEER_HEADS + h:at + PEER_HEADS + h + 1, ls], (rc_n, LANES))
          for h in range(PEER_HEADS)]
    for rc in range(nk // rc_n):
        rs = slice(rc * rc_n, (rc + 1) * rc_n)
        wg = None
        for h in range(PEER_HEADS):
            e = e2_ref[h, rs, ls]
            part = jnp.where(e >= thr[h], e, 0.0) * f1[h]
            wg = part if wg is None else wg + part
        es = slice(jj * nk + rc * rc_n, jj * nk + (rc + 1) * rc_n)
        wp_ref[es, ls] = (wg * _gelu(st_ref[es, ls])).astype(BF16)


def _peer_half_step(xb_ref, ut_ref, vt_ref, cols, e2_ref, rows_ref, base,
                    st_out_ref, st_in_ref, wp_out_ref, wp_in_ref, acc_ref):
    nk = e2_ref.shape[1]
    n_rows = st_in_ref.shape[0] // nk
    n_lc = st_in_ref.shape[1] // LANES
    rows_per = acc_ref.shape[0] // n_lc
    for lc in range(n_lc):
        ls = slice(lc * LANES, (lc + 1) * LANES)
        s = jnp.dot(xb_ref[ls, :], ut_ref[:, cols], preferred_element_type=F32)
        st_out_ref[:, ls] = s.T
        for jj in range(n_rows):
            _peer_gate_piece(e2_ref, rows_ref, base, st_in_ref, wp_out_ref, jj, ls)
        dr = slice(lc * rows_per, (lc + 1) * rows_per)
        acc_ref[dr, :] += jnp.dot(vt_ref[dr, cols], wp_in_ref[...], preferred_element_type=F32)


def _peer_dense_body(alpha, xb_ref, ut_ref, vt_ref, e2_ref, f1_ref, thr_ref, res_ref, g_ref, b_ref,
                     h_ref, hb_ref, acc_ref, st0_ref, st1_ref, wp0_ref, wp1_ref, rows_ref):
    j = pl.program_id(1)
    nk = e2_ref.shape[1]
    half = ut_ref.shape[1] // 2
    n_rows = half // nk

    @pl.when(j == 0)
    def _():
        acc_ref[...] = jnp.zeros(acc_ref.shape, F32)
        st1_ref[...] = jnp.zeros(st1_ref.shape, F32)
        wp0_ref[...] = jnp.zeros(wp0_ref.shape, BF16)

    per_half = n_rows * 2 * PEER_HEADS
    _peer_stage_rows(thr_ref, f1_ref, rows_ref, 0, 2 * j - 1, n_rows)
    _peer_stage_rows(thr_ref, f1_ref, rows_ref, per_half, 2 * j, n_rows)

    _peer_half_step(xb_ref, ut_ref, vt_ref, slice(0, half), e2_ref, rows_ref, 0,
                    st0_ref, st1_ref, wp1_ref, wp0_ref, acc_ref)
    _peer_half_step(xb_ref, ut_ref, vt_ref, slice(half, 2 * half), e2_ref, rows_ref, per_half,
                    st1_ref, st0_ref, wp0_ref, wp1_ref, acc_ref)

    @pl.when(j == pl.num_programs(1) - 1)
    def _():
        y = _layer_norm(alpha * res_ref[...] + acc_ref[...].T, g_ref[...], b_ref[...])
        h_ref[...] = y
        hb_ref[...] = y.astype(BF16)


def _peer_dense(xb, ut, vt, e2, f1, thr, res, g, b, alpha):
    t, d = xb.shape
    ne = ut.shape[1]
    nk = e2.shape[1]
    tt = min(PEER_TT, t)
    eb = PEER_EB
    hds = PEER_HEADS
    nj = ne // eb
    once = pl.Buffered(1)
    return pl.pallas_call(
        functools.partial(_peer_dense_body, alpha),
        grid=(t // tt, nj + 1),
        in_specs=[
            pl.BlockSpec((tt, d), lambda i, j: (i, 0)),
            pl.BlockSpec((d, eb), lambda i, j: (0, jnp.minimum(j, nj - 1))),
            pl.BlockSpec((d, eb), lambda i, j: (0, jnp.maximum(j - 1, 0))),
            pl.BlockSpec((hds, nk, tt), lambda i, j: (0, 0, i)),
            pl.BlockSpec((hds, nk, tt), lambda i, j: (0, 0, i), pipeline_mode=once),
            pl.BlockSpec((hds, nk, tt), lambda i, j: (0, 0, i), pipeline_mode=once),
            pl.BlockSpec((tt, d), lambda i, j: (i, 0), pipeline_mode=once),
            pl.BlockSpec((1, d), lambda i, j: (0, 0)),
            pl.BlockSpec((1, d), lambda i, j: (0, 0)),
        ],
        out_specs=[pl.BlockSpec((tt, d), lambda i, j: (i, 0)), pl.BlockSpec((tt, d), lambda i, j: (i, 0))],
        out_shape=[jax.ShapeDtypeStruct((t, d), F32), jax.ShapeDtypeStruct((t, d), BF16)],
        scratch_shapes=[
            pltpu.VMEM((d, tt), F32),
            pltpu.VMEM((eb // 2, tt), F32),
            pltpu.VMEM((eb // 2, tt), F32),
            pltpu.VMEM((eb // 2, tt), BF16),
            pltpu.VMEM((eb // 2, tt), BF16),
            pltpu.VMEM((2 * (eb // 2 // nk) * 2 * hds, tt), F32),
        ],
        compiler_params=_params(("arbitrary", "arbitrary")),
        name="peer_dense",
    )(xb, ut, vt, e2, f1, thr, res, g.reshape(1, d), b.reshape(1, d))


def kernel(x, p, ln_mix_g, ln_mix_b, ln_ffn_g, ln_ffn_b, pool_w_in, pool_w_group, pool_scale, pool_w_out, ssm_w_in, ssm_lambda_re, ssm_lambda_im, ssm_log_step, ssm_b_re, ssm_b_im, ssm_c_re, ssm_c_im, ssm_d, ssm_w_glu, ssm_w_out, attn_w_qkv, attn_rel_bias, attn_w_out, peer_w_q, peer_sub_keys, peer_u, peer_v, ple_w_proj, ple_w_gate, ple_b_gate):
    bsz, seq, d = x.shape
    depth = p.shape[0]
    t = bsz * seq
    alpha = (2.0 * depth) ** 0.25
    h = x.reshape(t, d)
    hb = h.astype(BF16)
    for i in range(depth):
        kind, j = i % N_MIXERS, i // N_MIXERS
        if kind == 0:
            mixed = _pool_mixer(hb, seq, pool_w_in[j].astype(BF16), pool_w_group[j].astype(BF16), pool_scale[j])
            w_out = pool_w_out[j]
        elif kind == 1:
            ops = _ssm_operators(ssm_lambda_re[j], ssm_lambda_im[j], ssm_log_step[j], ssm_b_re[j], ssm_b_im[j],
                                 ssm_c_re[j], ssm_c_im[j], ssm_d[j], SSM_LC, seq // SSM_LC)
            mixed = _ssm_mixer(hb, bsz, seq, ssm_w_in[j].astype(BF16), ops, ssm_w_glu[j].astype(BF16))
            w_out = ssm_w_out[j]
        else:
            qkv = _matmul(hb, attn_w_qkv[j].astype(BF16), BF16)
            bias_prev, bias_cur = _attn_tables(attn_rel_bias[j], ATTN_TQ)
            mixed = _chunk_attention(qkv, bsz, seq, ATTN_HEADS, bias_prev, bias_cur)
            w_out = attn_w_out[j]
        h, hb, ht = _matmul_ln(mixed, w_out.astype(BF16), h, ln_mix_g[i], ln_mix_b[i], alpha)
        qt = _matmul(peer_w_q[i].T.astype(BF16), ht, BF16)
        e2, f1, thr = _peer_scores(qt, peer_sub_keys[i].astype(BF16))
        h, hb = _peer_dense(hb, peer_u[i].T.astype(BF16), peer_v[i].T.astype(BF16), e2, f1, thr,
                            h, ln_ffn_g[i], ln_ffn_b[i], alpha)
        h, hb = _ple(hb, h, p[i].reshape(t, -1).astype(BF16), ple_w_gate[i].astype(BF16),
                     ple_w_proj[i].astype(BF16), ple_b_gate[i])
    return h.reshape(bsz, seq, d)
```

```python
import functools
import math

import jax
import jax.numpy as jnp
import numpy as np
from jax import lax
from jax.experimental import pallas as pl
from jax.experimental.pallas import tpu as pltpu

F32 = jnp.float32
BF16 = jnp.bfloat16

CHUNK = 64
N_MIXERS = 3
LN_EPS = 1e-5
POOL_WINDOWS = (2, 4, 8, 16)
POOL_MAX_WINDOW = max(POOL_WINDOWS)
SSM_GROUP_DIM = 16
SSM_STATE = 64
ATTN_HEADS = 16
ATTN_LEFT_CHUNKS = 8
ATTN_MAX_REL = 2 * CHUNK
NEG_INF = -1e30
PEER_N_KEYS = 128
PEER_HEADS = 8
PEER_HALF = 128
PEER_TOPK = 16

LANES = 128
VMEM_LIMIT = 56 * 1024 * 1024
ROW_TILE = 256
MM_TM, MM_TN = 512, 1024
ATTN_TQ = ATTN_LEFT_CHUNKS * CHUNK
PEER_TS = 8 * LANES
PEER_TT = 512
PEER_EB = 512
PEER_RC = 8
PEER_GATE_LANES = 2 * LANES
SSM_LC = 64


def _params(sem):
    return pltpu.CompilerParams(dimension_semantics=sem, vmem_limit_bytes=VMEM_LIMIT)


def _divisor_tile(n, preferred):
    best = LANES
    for cand in range(LANES, min(n, preferred) + 1, LANES):
        if n % cand == 0:
            best = cand
    assert n % best == 0, (n, preferred)
    return best


def _gelu(x):
    return 0.5 * x * (1.0 + lax.erf(x * (1.0 / math.sqrt(2.0))))


def _layer_norm(z, g, b):
    mu = jnp.mean(z, axis=-1, keepdims=True)
    zc = z - mu
    var = jnp.mean(zc * zc, axis=-1, keepdims=True)
    return zc * lax.rsqrt(var + LN_EPS) * g + b


def _mm_body(a_ref, w_ref, o_ref):
    o_ref[...] = jnp.dot(a_ref[...], w_ref[...], preferred_element_type=F32).astype(o_ref.dtype)


def _matmul(a, w, out_dtype):
    m, k = a.shape
    n = w.shape[1]
    tm, tn = _divisor_tile(m, MM_TM), _divisor_tile(n, MM_TN)
    return pl.pallas_call(
        _mm_body,
        grid=(n // tn, m // tm),
        in_specs=[pl.BlockSpec((tm, k), lambda j, i: (i, 0)), pl.BlockSpec((k, tn), lambda j, i: (0, j))],
        out_specs=pl.BlockSpec((tm, tn), lambda j, i: (i, j)),
        out_shape=jax.ShapeDtypeStruct((m, n), out_dtype),
        compiler_params=_params(("arbitrary", "arbitrary")),
        name="matmul",
    )(a, w)


def _mm_ln_body(alpha, a_ref, w_ref, res_ref, g_ref, b_ref, h_ref, hb_ref, ht_ref):
    m = jnp.dot(a_ref[...], w_ref[...], preferred_element_type=F32)
    y = _layer_norm(alpha * res_ref[...] + m, g_ref[...], b_ref[...])
    h_ref[...] = y
    hb_ref[...] = y.astype(BF16)
    ht_ref[...] = y.T.astype(BF16)


def _matmul_ln(a, w, res, g, b, alpha):
    t, k = a.shape
    d = w.shape[1]
    tm = min(ROW_TILE, t)
    return pl.pallas_call(
        functools.partial(_mm_ln_body, alpha),
        grid=(t // tm,),
        in_specs=[
            pl.BlockSpec((tm, k), lambda i: (i, 0)),
            pl.BlockSpec((k, d), lambda i: (0, 0)),
            pl.BlockSpec((tm, d), lambda i: (i, 0)),
            pl.BlockSpec((1, d), lambda i: (0, 0)),
            pl.BlockSpec((1, d), lambda i: (0, 0)),
        ],
        out_specs=[pl.BlockSpec((tm, d), lambda i: (i, 0)), pl.BlockSpec((tm, d), lambda i: (i, 0)),
                   pl.BlockSpec((d, tm), lambda i: (0, i))],
        out_shape=[jax.ShapeDtypeStruct((t, d), F32), jax.ShapeDtypeStruct((t, d), BF16),
                   jax.ShapeDtypeStruct((d, t), BF16)],
        compiler_params=_params(("arbitrary",)),
        name="outproj_ln",
    )(a, w, res, g.reshape(1, d), b.reshape(1, d))


def _ple_body(hb_ref, h_ref, p_ref, wg_ref, wp_ref, bg_ref, o_ref, ob_ref):
    gate = jax.nn.sigmoid(jnp.dot(hb_ref[...], wg_ref[...], preferred_element_type=F32) + bg_ref[...])
    proj = jnp.dot(p_ref[...], wp_ref[...], preferred_element_type=F32)
    y = h_ref[...] + gate * proj
    o_ref[...] = y
    ob_ref[...] = y.astype(BF16)


def _ple(hb, h, p, wg, wp, bg):
    t, d = h.shape
    pd = p.shape[1]
    tm = min(ROW_TILE, t)
    return pl.pallas_call(
        _ple_body,
        grid=(t // tm,),
        in_specs=[
            pl.BlockSpec((tm, d), lambda i: (i, 0)),
            pl.BlockSpec((tm, d), lambda i: (i, 0)),
            pl.BlockSpec((tm, pd), lambda i: (i, 0)),
            pl.BlockSpec((d, d), lambda i: (0, 0)),
            pl.BlockSpec((pd, d), lambda i: (0, 0)),
            pl.BlockSpec((1, d), lambda i: (0, 0)),
        ],
        out_specs=[pl.BlockSpec((tm, d), lambda i: (i, 0)), pl.BlockSpec((tm, d), lambda i: (i, 0))],
        out_shape=[jax.ShapeDtypeStruct((t, d), F32), jax.ShapeDtypeStruct((t, d), BF16)],
        compiler_params=_params(("arbitrary",)),
        name="ple_gate",
    )(hb, h, p, wg, wp, bg.reshape(1, d))


def _pool_body(tiles_per_seq, x_ref, win_ref, wg_ref, sc_ref, o_ref, ext_ref):
    wmax = POOL_MAX_WINDOW
    tm = x_ref.shape[0]
    gd = wg_ref.shape[1]
    i = pl.program_id(0)

    @pl.when(i % tiles_per_seq == 0)
    def _():
        ext_ref[0:wmax, :] = jnp.zeros((wmax, ext_ref.shape[1]), F32)

    ext_ref[wmax:wmax + tm, :] = jnp.dot(x_ref[...], win_ref[...], preferred_element_type=F32)
    pos = (i % tiles_per_seq) * tm + lax.broadcasted_iota(jnp.int32, (tm, 1), 0)
    for g, w in enumerate(POOL_WINDOWS):
        cols = slice(g * gd, (g + 1) * gd)
        u = ext_ref[wmax:wmax + tm, cols]
        acc = u
        for k in range(1, w):
            acc = acc + ext_ref[wmax - k:wmax - k + tm, cols]
        count = jnp.minimum(pos + 1, w).astype(F32)
        pooled = acc / count - u
        mixed = jnp.dot(pooled.astype(BF16), wg_ref[g], preferred_element_type=F32) * sc_ref[:, cols]
        o_ref[:, cols] = mixed.astype(BF16)
    ext_ref[0:wmax, :] = ext_ref[tm:tm + wmax, :]


def _pool_mixer(hb, seq, w_in, w_group, scale):
    t, d = hb.shape
    tm = min(ROW_TILE, seq)
    ng, gd = w_group.shape[0], w_group.shape[1]
    return pl.pallas_call(
        functools.partial(_pool_body, seq // tm),
        grid=(t // tm,),
        in_specs=[
            pl.BlockSpec((tm, d), lambda i: (i, 0)),
            pl.BlockSpec((d, d), lambda i: (0, 0)),
            pl.BlockSpec((ng, gd, gd), lambda i: (0, 0, 0)),
            pl.BlockSpec((1, d), lambda i: (0, 0)),
        ],
        out_specs=pl.BlockSpec((tm, d), lambda i: (i, 0)),
        out_shape=jax.ShapeDtypeStruct((t, d), BF16),
        scratch_shapes=[pltpu.VMEM((tm + POOL_MAX_WINDOW, d), F32)],
        compiler_params=_params(("arbitrary",)),
        name="pool_mixer",
    )(hb, w_in, w_group, scale.reshape(1, d))


def _cmul(ar, ai, br, bi):
    return ar * br - ai * bi, ar * bi + ai * br


def _ssm_operators(lam_re, lam_im, log_step, b_re, b_im, c_re, c_im, d_skip, lc, n_blocks):
    f32 = F32
    lr, li = lam_re.astype(f32), lam_im.astype(f32)
    dt = jnp.exp(log_step.astype(f32))[:, None]
    mag = jnp.exp(dt * lr)
    abar_r, abar_i = mag * jnp.cos(dt * li), mag * jnp.sin(dt * li)
    den = lr * lr + li * li
    k_r = ((abar_r - 1.0) * lr + abar_i * li) / den
    k_i = (abar_i * lr - (abar_r - 1.0) * li) / den
    bb_r, bb_i = _cmul(k_r[..., None], k_i[..., None], b_re.astype(f32), b_im.astype(f32))
    cr, ci = c_re.astype(f32), c_im.astype(f32)
    g_n, p_n = lr.shape
    c_n = bb_r.shape[-1]

    def pw_step(carry, _):
        pr, pi = carry
        nr, ni = _cmul(pr, pi, abar_r, abar_i)
        return (nr, ni), (pr, pi)

    (_, _), (pw_r, pw_i) = lax.scan(pw_step, (jnp.ones_like(abar_r), jnp.zeros_like(abar_i)), None, length=lc + 1)
    ca_r = cr[None] * pw_r[:, :, None, :] - ci[None] * pw_i[:, :, None, :]
    ca_i = cr[None] * pw_i[:, :, None, :] + ci[None] * pw_r[:, :, None, :]
    kern = jnp.einsum('tgop,gpc->tgoc', ca_r[:lc], bb_r) - jnp.einsum('tgop,gpc->tgoc', ca_i[:lc], bb_i)
    eye = jnp.eye(c_n, dtype=f32)
    kern = kern.at[0].add(d_skip.astype(f32).reshape(g_n, c_n)[:, :, None] * eye[None])
    s_idx = jnp.arange(lc)[:, None]
    t_idx = jnp.arange(lc)[None, :]
    lag = t_idx - s_idx
    toep = kern[jnp.clip(lag, 0, lc - 1)] * (lag >= 0)[:, :, None, None, None].astype(f32)
    intra = jnp.transpose(toep, (2, 0, 4, 1, 3)).reshape(g_n, lc * c_n, lc * c_n)
    rev_r, rev_i = pw_r[lc - 1 - jnp.arange(lc)], pw_i[lc - 1 - jnp.arange(lc)]
    ts_r, ts_i = _cmul(rev_r[..., None], rev_i[..., None], bb_r[None], bb_i[None])
    to_state = jnp.concatenate([jnp.transpose(ts_r, (1, 0, 3, 2)), jnp.transpose(ts_i, (1, 0, 3, 2))], axis=-1)
    to_state = to_state.reshape(g_n, lc * c_n, 2 * p_n)
    fs_r = jnp.transpose(ca_r[1:lc + 1], (1, 3, 0, 2)).reshape(g_n, p_n, lc * c_n)
    fs_i = jnp.transpose(ca_i[1:lc + 1], (1, 3, 0, 2)).reshape(g_n, p_n, lc * c_n)
    from_state = jnp.concatenate([fs_r, -fs_i], axis=1)
    n_steps = max(1, (n_blocks - 1).bit_length())
    rows = []
    ar, ai = pw_r[lc], pw_i[lc]
    for _ in range(n_steps):
        rows.append(jnp.stack([jnp.concatenate([ar, ar], -1), jnp.concatenate([-ai, ai], -1)], axis=1))
        ar, ai = _cmul(ar, ai, ar, ai)
    apow = jnp.stack(rows, axis=1)
    return intra.astype(BF16), to_state.astype(BF16), from_state.astype(BF16), apow


def _ssm_body(n_blocks, u_ref, intra_ref, ts_ref, fs_ref, apow_ref, o_ref):
    u = u_ref[0]
    y = jnp.dot(u, intra_ref[0], preferred_element_type=F32)
    hs = jnp.dot(u, ts_ref[0], preferred_element_type=F32)
    rows = hs.shape[0]
    half = hs.shape[1] // 2
    blk = lax.broadcasted_iota(jnp.int32, (rows, 1), 0) % n_blocks
    d = 1
    k = 0
    while d < n_blocks:
        sh = jnp.where(blk >= d, pltpu.roll(hs, d, axis=0), 0.0)
        hs = hs + apow_ref[0, k, 0:1, :] * sh + apow_ref[0, k, 1:2, :] * pltpu.roll(sh, half, axis=1)
        d *= 2
        k += 1
    h_in = jnp.where(blk >= 1, pltpu.roll(hs, 1, axis=0), 0.0)
    y = y + jnp.dot(h_in.astype(BF16), fs_ref[0], preferred_element_type=F32)
    o_ref[0] = _gelu(y).astype(BF16)


def _ssm_scan(u_g, intra, to_state, from_state, apow, n_blocks):
    g_n, rows, width = u_g.shape
    p2 = to_state.shape[2]
    n_steps = apow.shape[1]
    return pl.pallas_call(
        functools.partial(_ssm_body, n_blocks),
        grid=(g_n,),
        in_specs=[
            pl.BlockSpec((1, rows, width), lambda g: (g, 0, 0)),
            pl.BlockSpec((1, width, width), lambda g: (g, 0, 0)),
            pl.BlockSpec((1, width, p2), lambda g: (g, 0, 0)),
            pl.BlockSpec((1, p2, width), lambda g: (g, 0, 0)),
            pl.BlockSpec((1, n_steps, 2, p2), lambda g: (g, 0, 0, 0)),
        ],
        out_specs=pl.BlockSpec((1, rows, width), lambda g: (g, 0, 0)),
        out_shape=jax.ShapeDtypeStruct((g_n, rows, width), BF16),
        compiler_params=_params(("arbitrary",)),
        name="ssm_scan",
    )(u_g, intra, to_state, from_state, apow)


def _glu_body(g_ref, w_ref, o_ref):
    g = g_ref[...]
    z = jnp.dot(g, w_ref[...], preferred_element_type=F32)
    o_ref[...] = (g.astype(F32) * jax.nn.sigmoid(z)).astype(BF16)


def _glu(g, w):
    t, d = g.shape
    tm = min(ROW_TILE, t)
    return pl.pallas_call(
        _glu_body,
        grid=(t // tm,),
        in_specs=[pl.BlockSpec((tm, d), lambda i: (i, 0)), pl.BlockSpec((d, d), lambda i: (0, 0))],
        out_specs=pl.BlockSpec((tm, d), lambda i: (i, 0)),
        out_shape=jax.ShapeDtypeStruct((t, d), BF16),
        compiler_params=_params(("arbitrary",)),
        name="ssm_glu",
    )(g, w)


def _ssm_mixer(hb, bsz, seq, w_in, ops, w_glu):
    t, d = hb.shape
    lc = SSM_LC
    n_blocks = seq // lc
    g_n = d // SSM_GROUP_DIM
    u = _matmul(hb, w_in, BF16)
    u_g = u.reshape(bsz * n_blocks, lc, g_n, SSM_GROUP_DIM).transpose(2, 0, 1, 3).reshape(g_n, bsz * n_blocks, lc * SSM_GROUP_DIM)
    y_g = _ssm_scan(u_g, *ops, n_blocks)
    gl = y_g.reshape(g_n, bsz * n_blocks, lc, SSM_GROUP_DIM).transpose(1, 2, 0, 3).reshape(t, d)
    return _glu(gl, w_glu)


def _attn_tables(rel_bias, tq):
    rb = rel_bias.astype(F32)
    heads = rb.shape[0]
    period = 2 * tq - 1

    def toeplitz(offset):
        m = np.arange(period)
        diff = np.where(m < tq, -m, period - m) + offset
        diag = rb[:, np.clip(diff, -ATTN_MAX_REL, ATTN_MAX_REL) + ATTN_MAX_REL]
        rows = jnp.tile(diag, (1, tq))[:, :tq * (period - 1)].reshape(heads, tq, period - 1)
        return rows[:, :, :tq]

    qc = np.arange(tq)[:, None] // CHUNK
    kc = np.arange(tq)[None, :] // CHUNK
    cur = jnp.where((kc <= qc)[None], toeplitz(0), NEG_INF)
    prev = jnp.where((kc >= qc)[None], toeplitz(tq), NEG_INF)
    return prev, cur


def _attn_body(scale, q_ref, kp_ref, kc_ref, vp_ref, vc_ref, bp_ref, bc_ref, o_ref):
    i = pl.program_id(2)
    q = q_ref[...]
    nt = (((1,), (1,)), ((), ()))
    s_p = lax.dot_general(q, kp_ref[...], nt, preferred_element_type=F32) * scale + bp_ref[0]
    s_c = lax.dot_general(q, kc_ref[...], nt, preferred_element_type=F32) * scale + bc_ref[0]
    s_p = jnp.where(i == 0, NEG_INF, s_p)
    m = jnp.maximum(jnp.max(s_p, axis=-1, keepdims=True), jnp.max(s_c, axis=-1, keepdims=True))
    p_p = jnp.exp(s_p - m)
    p_c = jnp.exp(s_c - m)
    l = jnp.sum(p_p, axis=-1, keepdims=True) + jnp.sum(p_c, axis=-1, keepdims=True)
    o = jnp.dot(p_p.astype(BF16), vp_ref[...], preferred_element_type=F32)
    o = o + jnp.dot(p_c.astype(BF16), vc_ref[...], preferred_element_type=F32)
    o_ref[...] = (o / l).astype(BF16)


def _chunk_attention(qkv, bsz, seq, heads, bias_prev, bias_cur):
    t = qkv.shape[0]
    hd = qkv.shape[1] // (3 * heads)
    tq = ATTN_TQ
    nt = seq // tq
    cur = lambda h, b, i: (b * nt + i, 0)
    prev = lambda h, b, i: (b * nt + jnp.maximum(i - 1, 0), 0)

    def col(base, rows):
        return lambda h, b, i: (rows(h, b, i)[0], base + h)

    return pl.pallas_call(
        functools.partial(_attn_body, hd ** -0.5),
        grid=(heads, bsz, nt),
        in_specs=[
            pl.BlockSpec((tq, hd), col(0, cur)),
            pl.BlockSpec((tq, hd), col(heads, prev)),
            pl.BlockSpec((tq, hd), col(heads, cur)),
            pl.BlockSpec((tq, hd), col(2 * heads, prev)),
            pl.BlockSpec((tq, hd), col(2 * heads, cur)),
            pl.BlockSpec((1, tq, tq), lambda h, b, i: (h, 0, 0)),
            pl.BlockSpec((1, tq, tq), lambda h, b, i: (h, 0, 0)),
        ],
        out_specs=pl.BlockSpec((tq, hd), lambda h, b, i: (b * nt + i, h)),
        out_shape=jax.ShapeDtypeStruct((t, heads * hd), BF16),
        compiler_params=_params(("arbitrary", "arbitrary", "arbitrary")),
        name="chunk_attention",
    )(qkv, qkv, qkv, qkv, qkv, bias_prev, bias_cur)


def _sort_pairs(n):
    pairs = []
    p = 1
    while p < n:
        k = p
        while k >= 1:
            for j in range(k % p, n - k, 2 * k):
                for i in range(min(k, n - j - k)):
                    if (i + j) // (2 * p) == (i + j + k) // (2 * p):
                        pairs.append((i + j, i + j + k))
            k //= 2
        p *= 2
    return pairs


def _cmp_exchange(x, y):
    if x is None:
        return y, None
    if y is None:
        return x, None
    return jnp.maximum(x, y), jnp.minimum(x, y)


def _opt_max(x, y):
    if x is None:
        return y
    if y is None:
        return x
    return jnp.maximum(x, y)


def _merge_top(a, a_next, b, b_next):
    k = len(a)
    hi, lo_max = [], None
    for i in range(k):
        h, l = _cmp_exchange(a[i], b[k - 1 - i])
        hi.append(h)
        lo_max = _opt_max(lo_max, l)
    d = k // 2
    while d >= 1:
        for i in range(k):
            if i & d == 0:
                hi[i], hi[i + d] = _cmp_exchange(hi[i], hi[i + d])
        d //= 2
    return hi, _opt_max(_opt_max(lo_max, a_next), b_next)


def _top_sorted(vals, k):
    vals = list(vals) + [None] * (-len(vals) % k)
    pairs = _sort_pairs(k)
    groups = []
    for g in range(0, len(vals), k):
        v = vals[g:g + k]
        for i, j in pairs:
            v[i], v[j] = _cmp_exchange(v[i], v[j])
        groups.append((v, None))
    while len(groups) > 1:
        merged = [_merge_top(*groups[i], *groups[i + 1]) for i in range(0, len(groups) - 1, 2)]
        groups = merged + groups[len(groups) - len(groups) % 2:]
    return groups[0]


def _peer_score_body(q_ref, keys_ref, e2_ref, f1_ref, thr_ref, km1_scr, km2_scr, stat_scr):
    k = PEER_TOPK
    nk = keys_ref.shape[1]
    half = keys_ref.shape[2]
    ts = q_ref.shape[1]
    groups = ts // LANES
    assert groups == 8, "one sublane per 128-token group"

    def head(h, carry):
        base = pl.multiple_of(h * 2 * half, 2 * half)
        s1 = jnp.dot(keys_ref[0], q_ref[pl.ds(base, half), :], preferred_element_type=F32)
        s2 = jnp.dot(keys_ref[1], q_ref[pl.ds(base + half, half), :], preferred_element_type=F32)
        for g in range(groups):
            km1_scr[g * nk:(g + 1) * nk, :] = s1[:, g * LANES:(g + 1) * LANES]
            km2_scr[g * nk:(g + 1) * nk, :] = s2[:, g * LANES:(g + 1) * LANES]
        v1 = [km1_scr[pl.ds(key, groups, stride=nk), :] for key in range(nk)]
        v2 = [km2_scr[pl.ds(key, groups, stride=nk), :] for key in range(nk)]
        a, a_next = _top_sorted(v1, k)
        b, b_next = _top_sorted(v2, k)
        cand = [a[i] + b[j] for i in range(k) for j in range(k) if (i + 1) * (j + 1) <= k]
        cand += [a_next + b[0], a[0] + b_next]
        best, nxt = _top_sorted(cand, k)
        z = 1.0 + jnp.exp(best[1] - best[0])
        for r in range(2, k):
            z = z + jnp.exp(best[r] - best[0])
        stat_scr[0 * groups:1 * groups, :] = a[0]
        stat_scr[1 * groups:2 * groups, :] = b[0]
        stat_scr[2 * groups:3 * groups, :] = z
        stat_scr[3 * groups:4 * groups, :] = 0.5 * (best[k - 1] + nxt)
        for g in range(groups):
            cols = slice(g * LANES, (g + 1) * LANES)
            m1 = stat_scr[0 * groups + g:0 * groups + g + 1, :]
            m2 = stat_scr[1 * groups + g:1 * groups + g + 1, :]
            zg = stat_scr[2 * groups + g:2 * groups + g + 1, :]
            tau = stat_scr[3 * groups + g:3 * groups + g + 1, :]
            s1g = km1_scr[g * nk:(g + 1) * nk, :]
            s2g = km2_scr[g * nk:(g + 1) * nk, :]
            e2_ref[h, :, cols] = jnp.exp(s2g - m2).astype(BF16)
            f1_ref[h, :, cols] = jnp.exp(s1g - m1) / zg
            thr_ref[h, :, cols] = jnp.exp(tau - s1g - m2)
        return carry

    lax.fori_loop(0, PEER_HEADS, head, 0)


def _peer_scores(qt, keys):
    qd, t = qt.shape
    ts = PEER_TS
    nk = keys.shape[1]
    groups = ts // LANES
    spec = pl.BlockSpec((PEER_HEADS, nk, ts), lambda i: (0, 0, i))
    shape = jax.ShapeDtypeStruct((PEER_HEADS, nk, t), F32)
    return pl.pallas_call(
        _peer_score_body,
        grid=(t // ts,),
        in_specs=[
            pl.BlockSpec((qd, ts), lambda i: (0, i)),
            pl.BlockSpec((2, nk, PEER_HALF), lambda i: (0, 0, 0)),
        ],
        out_specs=[spec, spec, spec],
        out_shape=[jax.ShapeDtypeStruct(shape.shape, BF16), shape, shape],
        scratch_shapes=[
            pltpu.VMEM((groups * nk, LANES), F32),
            pltpu.VMEM((groups * nk, LANES), F32),
            pltpu.VMEM((4 * groups, LANES), F32),
        ],
        compiler_params=_params(("arbitrary",)),
        name="peer_scores",
    )(qt, keys)


def _peer_stage_rows(thr_ref, f1_ref, rows_ref, base, blk, n_rows):
    last_row = thr_ref.shape[1] - 1
    for jj in range(n_rows):
        row = jnp.clip(blk * n_rows + jj, 0, last_row)
        for h in range(PEER_HEADS):
            at = base + (jj * 2) * PEER_HEADS + h
            rows_ref[at:at + 1, :] = thr_ref[h, pl.ds(row, 1), :]
            rows_ref[at + PEER_HEADS:at + PEER_HEADS + 1, :] = f1_ref[h, pl.ds(row, 1), :]


def _peer_gate_piece(e2_ref, rows_ref, base, st_ref, wp_ref, jj, ls):
    nk = e2_ref.shape[1]
    rc_n = PEER_RC
    width = ls.stop - ls.start
    at = base + (jj * 2) * PEER_HEADS
    thr = [jnp.broadcast_to(rows_ref[at + h:at + h + 1, ls], (rc_n, width)).astype(BF16) for h in range(PEER_HEADS)]
    f1 = [jnp.broadcast_to(rows_ref[at + PEER_HEADS + h:at + PEER_HEADS + h + 1, ls], (rc_n, width)).astype(BF16)
          for h in range(PEER_HEADS)]
    zero = jnp.zeros((rc_n, width), BF16)
    for rc in range(nk // rc_n):
        rs = slice(rc * rc_n, (rc + 1) * rc_n)
        wg = None
        for h in range(PEER_HEADS):
            e = e2_ref[h, rs, ls]
            part = jnp.where(e >= thr[h], e, zero) * f1[h]
            wg = part if wg is None else wg + part
        es = slice(jj * nk + rc * rc_n, jj * nk + (rc + 1) * rc_n)
        wp_ref[es, ls] = wg * _gelu(st_ref[es, ls]).astype(BF16)


def _peer_half_step(xb_ref, ut_ref, vt_ref, cols, e2_ref, rows_ref, base,
                    st_out_ref, st_in_ref, wp_out_ref, wp_in_ref, acc_ref):
    nk = e2_ref.shape[1]
    n_rows = st_in_ref.shape[0] // nk
    n_lc = st_in_ref.shape[1] // LANES
    rows_per = acc_ref.shape[0] // n_lc
    gate_cols = PEER_GATE_LANES // LANES
    for lc in range(n_lc):
        ls = slice(lc * LANES, (lc + 1) * LANES)
        s = jnp.dot(xb_ref[ls, :], ut_ref[0, :, cols], preferred_element_type=F32)
        st_out_ref[:, ls] = s.T
        if lc % gate_cols == 0:
            for jj in range(n_rows):
                _peer_gate_piece(e2_ref, rows_ref, base, st_in_ref, wp_out_ref, jj,
                                 slice(lc * LANES, lc * LANES + PEER_GATE_LANES))
        dr = slice(lc * rows_per, (lc + 1) * rows_per)
        acc_ref[dr, :] += jnp.dot(vt_ref[0, dr, cols], wp_in_ref[...], preferred_element_type=F32)


def _peer_dense_body(alpha, xb_ref, ut_ref, vt_ref, e2_ref, f1_ref, thr_ref, res_ref, g_ref, b_ref,
                     h_ref, hb_ref, acc_ref, st0_ref, st1_ref, wp0_ref, wp1_ref, rows_ref):
    j = pl.program_id(1)
    nk = e2_ref.shape[1]
    half = ut_ref.shape[2] // 2
    n_rows = half // nk

    @pl.when(j == 0)
    def _():
        acc_ref[...] = jnp.zeros(acc_ref.shape, F32)
        st1_ref[...] = jnp.zeros(st1_ref.shape, F32)
        wp0_ref[...] = jnp.zeros(wp0_ref.shape, BF16)

    per_half = n_rows * 2 * PEER_HEADS
    _peer_stage_rows(thr_ref, f1_ref, rows_ref, 0, 2 * j - 1, n_rows)
    _peer_stage_rows(thr_ref, f1_ref, rows_ref, per_half, 2 * j, n_rows)

    _peer_half_step(xb_ref, ut_ref, vt_ref, slice(0, half), e2_ref, rows_ref, 0,
                    st0_ref, st1_ref, wp1_ref, wp0_ref, acc_ref)
    _peer_half_step(xb_ref, ut_ref, vt_ref, slice(half, 2 * half), e2_ref, rows_ref, per_half,
                    st1_ref, st0_ref, wp0_ref, wp1_ref, acc_ref)

    @pl.when(j == pl.num_programs(1) - 1)
    def _():
        y = _layer_norm(alpha * res_ref[...] + acc_ref[...].T, g_ref[...], b_ref[...])
        h_ref[...] = y
        hb_ref[...] = y.astype(BF16)


def _peer_dense(xb, ut, vt, e2, f1, thr, res, g, b, alpha):
    t, d = xb.shape
    nj, _, eb = ut.shape
    nk = e2.shape[1]
    tt = min(PEER_TT, t)
    hds = PEER_HEADS
    once = pl.Buffered(1)
    return pl.pallas_call(
        functools.partial(_peer_dense_body, alpha),
        grid=(t // tt, nj + 1),
        in_specs=[
            pl.BlockSpec((tt, d), lambda i, j: (i, 0)),
            pl.BlockSpec((1, d, eb), lambda i, j: (jnp.minimum(j, nj - 1), 0, 0)),
            pl.BlockSpec((1, d, eb), lambda i, j: (jnp.maximum(j - 1, 0), 0, 0)),
            pl.BlockSpec((hds, nk, tt), lambda i, j: (0, 0, i)),
            pl.BlockSpec((hds, nk, tt), lambda i, j: (0, 0, i), pipeline_mode=once),
            pl.BlockSpec((hds, nk, tt), lambda i, j: (0, 0, i), pipeline_mode=once),
            pl.BlockSpec((tt, d), lambda i, j: (i, 0), pipeline_mode=once),
            pl.BlockSpec((1, d), lambda i, j: (0, 0)),
            pl.BlockSpec((1, d), lambda i, j: (0, 0)),
        ],
        out_specs=[pl.BlockSpec((tt, d), lambda i, j: (i, 0)), pl.BlockSpec((tt, d), lambda i, j: (i, 0))],
        out_shape=[jax.ShapeDtypeStruct((t, d), F32), jax.ShapeDtypeStruct((t, d), BF16)],
        scratch_shapes=[
            pltpu.VMEM((d, tt), F32),
            pltpu.VMEM((eb // 2, tt), F32),
            pltpu.VMEM((eb // 2, tt), F32),
            pltpu.VMEM((eb // 2, tt), BF16),
            pltpu.VMEM((eb // 2, tt), BF16),
            pltpu.VMEM((2 * (eb // 2 // nk) * 2 * hds, tt), F32),
        ],
        compiler_params=_params(("arbitrary", "arbitrary")),
        name="peer_dense",
    )(xb, ut, vt, e2, f1, thr, res, g.reshape(1, d), b.reshape(1, d))


def _expert_blocks(table):
    ne, d = table.shape
    return table.astype(BF16).reshape(ne // PEER_EB, PEER_EB, d).transpose(0, 2, 1)


def kernel(x, p, ln_mix_g, ln_mix_b, ln_ffn_g, ln_ffn_b, pool_w_in, pool_w_group, pool_scale, pool_w_out, ssm_w_in, ssm_lambda_re, ssm_lambda_im, ssm_log_step, ssm_b_re, ssm_b_im, ssm_c_re, ssm_c_im, ssm_d, ssm_w_glu, ssm_w_out, attn_w_qkv, attn_rel_bias, attn_w_out, peer_w_q, peer_sub_keys, peer_u, peer_v, ple_w_proj, ple_w_gate, ple_b_gate):
    bsz, seq, d = x.shape
    depth = p.shape[0]
    t = bsz * seq
    alpha = (2.0 * depth) ** 0.25
    h = x.reshape(t, d)
    hb = h.astype(BF16)
    for i in range(depth):
        kind, j = i % N_MIXERS, i // N_MIXERS
        if kind == 0:
            mixed = _pool_mixer(hb, seq, pool_w_in[j].astype(BF16), pool_w_group[j].astype(BF16), pool_scale[j])
            w_out = pool_w_out[j]
        elif kind == 1:
            ops = _ssm_operators(ssm_lambda_re[j], ssm_lambda_im[j], ssm_log_step[j], ssm_b_re[j], ssm_b_im[j],
                                 ssm_c_re[j], ssm_c_im[j], ssm_d[j], SSM_LC, seq // SSM_LC)
            mixed = _ssm_mixer(hb, bsz, seq, ssm_w_in[j].astype(BF16), ops, ssm_w_glu[j].astype(BF16))
            w_out = ssm_w_out[j]
        else:
            qkv = _matmul(hb, attn_w_qkv[j].astype(BF16), BF16)
            bias_prev, bias_cur = _attn_tables(attn_rel_bias[j], ATTN_TQ)
            mixed = _chunk_attention(qkv, bsz, seq, ATTN_HEADS, bias_prev, bias_cur)
            w_out = attn_w_out[j]
        h, hb, ht = _matmul_ln(mixed, w_out.astype(BF16), h, ln_mix_g[i], ln_mix_b[i], alpha)
        qt = _matmul(peer_w_q[i].T.astype(BF16), ht, BF16)
        e2, f1, thr = _peer_scores(qt, peer_sub_keys[i].astype(BF16))
        h, hb = _peer_dense(hb, _expert_blocks(peer_u[i]), _expert_blocks(peer_v[i]), e2, f1, thr,
                            h, ln_ffn_g[i], ln_ffn_b[i], alpha)
        h, hb = _ple(hb, h, p[i].reshape(t, -1).astype(BF16), ple_w_gate[i].astype(BF16),
                     ple_w_proj[i].astype(BF16), ple_b_gate[i])
    return h.reshape(bsz, seq, d)
```

```python
import functools
import math

import jax
import jax.numpy as jnp
import numpy as np
from jax import lax
from jax.experimental import pallas as pl
from jax.experimental.pallas import tpu as pltpu

F32 = jnp.float32
BF16 = jnp.bfloat16

CHUNK = 64
N_MIXERS = 3
LN_EPS = 1e-5
POOL_WINDOWS = (2, 4, 8, 16)
POOL_MAX_WINDOW = max(POOL_WINDOWS)
SSM_GROUP_DIM = 16
SSM_STATE = 64
ATTN_HEADS = 16
ATTN_LEFT_CHUNKS = 8
ATTN_MAX_REL = 2 * CHUNK
NEG_INF = -1e30
PEER_N_KEYS = 128
PEER_HEADS = 8
PEER_HALF = 128
PEER_TOPK = 16

LANES = 128
VMEM_LIMIT = 56 * 1024 * 1024
ROW_TILE = 256
MM_TM, MM_TN = 512, 1024
ATTN_TQ = ATTN_LEFT_CHUNKS * CHUNK
PEER_TS = 8 * LANES
PEER_TT = 512
PEER_EB = 512
PEER_RC = 8
PEER_GATE_LANES = 2 * LANES
SSM_LC = 32


def _params(sem):
    return pltpu.CompilerParams(dimension_semantics=sem, vmem_limit_bytes=VMEM_LIMIT)


def _divisor_tile(n, preferred):
    best = LANES
    for cand in range(LANES, min(n, preferred) + 1, LANES):
        if n % cand == 0:
            best = cand
    assert n % best == 0, (n, preferred)
    return best


def _gelu(x):
    return 0.5 * x * (1.0 + lax.erf(x * (1.0 / math.sqrt(2.0))))


def _layer_norm(z, g, b):
    mu = jnp.mean(z, axis=-1, keepdims=True)
    zc = z - mu
    var = jnp.mean(zc * zc, axis=-1, keepdims=True)
    return zc * lax.rsqrt(var + LN_EPS) * g + b


def _mm_body(a_ref, w_ref, o_ref):
    o_ref[...] = jnp.dot(a_ref[...], w_ref[...], preferred_element_type=F32).astype(o_ref.dtype)


def _matmul(a, w, out_dtype):
    m, k = a.shape
    n = w.shape[1]
    tm, tn = _divisor_tile(m, MM_TM), _divisor_tile(n, MM_TN)
    return pl.pallas_call(
        _mm_body,
        grid=(n // tn, m // tm),
        in_specs=[pl.BlockSpec((tm, k), lambda j, i: (i, 0)), pl.BlockSpec((k, tn), lambda j, i: (0, j))],
        out_specs=pl.BlockSpec((tm, tn), lambda j, i: (i, j)),
        out_shape=jax.ShapeDtypeStruct((m, n), out_dtype),
        compiler_params=_params(("arbitrary", "arbitrary")),
        name="matmul",
    )(a, w)


def _mm_ln_body(alpha, a_ref, w_ref, res_ref, g_ref, b_ref, h_ref, hb_ref, ht_ref):
    m = jnp.dot(a_ref[...], w_ref[...], preferred_element_type=F32)
    y = _layer_norm(alpha * res_ref[...] + m, g_ref[...], b_ref[...])
    h_ref[...] = y
    hb_ref[...] = y.astype(BF16)
    ht_ref[...] = y.T.astype(BF16)


def _matmul_ln(a, w, res, g, b, alpha):
    t, k = a.shape
    d = w.shape[1]
    tm = min(ROW_TILE, t)
    return pl.pallas_call(
        functools.partial(_mm_ln_body, alpha),
        grid=(t // tm,),
        in_specs=[
            pl.BlockSpec((tm, k), lambda i: (i, 0)),
            pl.BlockSpec((k, d), lambda i: (0, 0)),
            pl.BlockSpec((tm, d), lambda i: (i, 0)),
            pl.BlockSpec((1, d), lambda i: (0, 0)),
            pl.BlockSpec((1, d), lambda i: (0, 0)),
        ],
        out_specs=[pl.BlockSpec((tm, d), lambda i: (i, 0)), pl.BlockSpec((tm, d), lambda i: (i, 0)),
                   pl.BlockSpec((d, tm), lambda i: (0, i))],
        out_shape=[jax.ShapeDtypeStruct((t, d), F32), jax.ShapeDtypeStruct((t, d), BF16),
                   jax.ShapeDtypeStruct((d, t), BF16)],
        compiler_params=_params(("arbitrary",)),
        name="outproj_ln",
    )(a, w, res, g.reshape(1, d), b.reshape(1, d))


def _ple_body(hb_ref, h_ref, p_ref, wg_ref, wp_ref, bg_ref, o_ref, ob_ref):
    gate = jax.nn.sigmoid(jnp.dot(hb_ref[...], wg_ref[...], preferred_element_type=F32) + bg_ref[...])
    proj = jnp.dot(p_ref[...], wp_ref[...], preferred_element_type=F32)
    y = h_ref[...] + gate * proj
    o_ref[...] = y
    ob_ref[...] = y.astype(BF16)


def _ple(hb, h, p, wg, wp, bg):
    t, d = h.shape
    pd = p.shape[1]
    tm = min(ROW_TILE, t)
    return pl.pallas_call(
        _ple_body,
        grid=(t // tm,),
        in_specs=[
            pl.BlockSpec((tm, d), lambda i: (i, 0)),
            pl.BlockSpec((tm, d), lambda i: (i, 0)),
            pl.BlockSpec((tm, pd), lambda i: (i, 0)),
            pl.BlockSpec((d, d), lambda i: (0, 0)),
            pl.BlockSpec((pd, d), lambda i: (0, 0)),
            pl.BlockSpec((1, d), lambda i: (0, 0)),
        ],
        out_specs=[pl.BlockSpec((tm, d), lambda i: (i, 0)), pl.BlockSpec((tm, d), lambda i: (i, 0))],
        out_shape=[jax.ShapeDtypeStruct((t, d), F32), jax.ShapeDtypeStruct((t, d), BF16)],
        compiler_params=_params(("arbitrary",)),
        name="ple_gate",
    )(hb, h, p, wg, wp, bg.reshape(1, d))


def _pool_body(tiles_per_seq, x_ref, win_ref, wg_ref, sc_ref, o_ref, ext_ref):
    wmax = POOL_MAX_WINDOW
    tm = x_ref.shape[0]
    gd = wg_ref.shape[1]
    i = pl.program_id(0)

    @pl.when(i % tiles_per_seq == 0)
    def _():
        ext_ref[0:wmax, :] = jnp.zeros((wmax, ext_ref.shape[1]), F32)

    ext_ref[wmax:wmax + tm, :] = jnp.dot(x_ref[...], win_ref[...], preferred_element_type=F32)
    pos = (i % tiles_per_seq) * tm + lax.broadcasted_iota(jnp.int32, (tm, 1), 0)
    for g, w in enumerate(POOL_WINDOWS):
        cols = slice(g * gd, (g + 1) * gd)
        u = ext_ref[wmax:wmax + tm, cols]
        acc = u
        for k in range(1, w):
            acc = acc + ext_ref[wmax - k:wmax - k + tm, cols]
        count = jnp.minimum(pos + 1, w).astype(F32)
        pooled = acc / count - u
        mixed = jnp.dot(pooled.astype(BF16), wg_ref[g], preferred_element_type=F32) * sc_ref[:, cols]
        o_ref[:, cols] = mixed.astype(BF16)
    ext_ref[0:wmax, :] = ext_ref[tm:tm + wmax, :]


def _pool_mixer(hb, seq, w_in, w_group, scale):
    t, d = hb.shape
    tm = min(ROW_TILE, seq)
    ng, gd = w_group.shape[0], w_group.shape[1]
    return pl.pallas_call(
        functools.partial(_pool_body, seq // tm),
        grid=(t // tm,),
        in_specs=[
            pl.BlockSpec((tm, d), lambda i: (i, 0)),
            pl.BlockSpec((d, d), lambda i: (0, 0)),
            pl.BlockSpec((ng, gd, gd), lambda i: (0, 0, 0)),
            pl.BlockSpec((1, d), lambda i: (0, 0)),
        ],
        out_specs=pl.BlockSpec((tm, d), lambda i: (i, 0)),
        out_shape=jax.ShapeDtypeStruct((t, d), BF16),
        scratch_shapes=[pltpu.VMEM((tm + POOL_MAX_WINDOW, d), F32)],
        compiler_params=_params(("arbitrary",)),
        name="pool_mixer",
    )(hb, w_in, w_group, scale.reshape(1, d))


def _cmul(ar, ai, br, bi):
    return ar * br - ai * bi, ar * bi + ai * br


def _ssm_operators(lam_re, lam_im, log_step, b_re, b_im, c_re, c_im, d_skip, lc, n_blocks):
    f32 = F32
    lr, li = lam_re.astype(f32), lam_im.astype(f32)
    dt = jnp.exp(log_step.astype(f32))[:, None]
    mag = jnp.exp(dt * lr)
    abar_r, abar_i = mag * jnp.cos(dt * li), mag * jnp.sin(dt * li)
    den = lr * lr + li * li
    k_r = ((abar_r - 1.0) * lr + abar_i * li) / den
    k_i = (abar_i * lr - (abar_r - 1.0) * li) / den
    bb_r, bb_i = _cmul(k_r[..., None], k_i[..., None], b_re.astype(f32), b_im.astype(f32))
    cr, ci = c_re.astype(f32), c_im.astype(f32)
    g_n, p_n = lr.shape
    c_n = bb_r.shape[-1]

    def pw_step(carry, _):
        pr, pi = carry
        nr, ni = _cmul(pr, pi, abar_r, abar_i)
        return (nr, ni), (pr, pi)

    (_, _), (pw_r, pw_i) = lax.scan(pw_step, (jnp.ones_like(abar_r), jnp.zeros_like(abar_i)), None, length=lc + 1)
    ca_r = cr[None] * pw_r[:, :, None, :] - ci[None] * pw_i[:, :, None, :]
    ca_i = cr[None] * pw_i[:, :, None, :] + ci[None] * pw_r[:, :, None, :]
    kern = jnp.einsum('tgop,gpc->tgoc', ca_r[:lc], bb_r) - jnp.einsum('tgop,gpc->tgoc', ca_i[:lc], bb_i)
    eye = jnp.eye(c_n, dtype=f32)
    kern = kern.at[0].add(d_skip.astype(f32).reshape(g_n, c_n)[:, :, None] * eye[None])
    s_idx = jnp.arange(lc)[:, None]
    t_idx = jnp.arange(lc)[None, :]
    lag = t_idx - s_idx
    toep = kern[jnp.clip(lag, 0, lc - 1)] * (lag >= 0)[:, :, None, None, None].astype(f32)
    intra = jnp.transpose(toep, (2, 0, 4, 1, 3)).reshape(g_n, lc * c_n, lc * c_n)
    rev_r, rev_i = pw_r[lc - 1 - jnp.arange(lc)], pw_i[lc - 1 - jnp.arange(lc)]
    ts_r, ts_i = _cmul(rev_r[..., None], rev_i[..., None], bb_r[None], bb_i[None])
    to_state = jnp.concatenate([jnp.transpose(ts_r, (1, 0, 3, 2)), jnp.transpose(ts_i, (1, 0, 3, 2))], axis=-1)
    to_state = to_state.reshape(g_n, lc * c_n, 2 * p_n)
    fs_r = jnp.transpose(ca_r[1:lc + 1], (1, 3, 0, 2)).reshape(g_n, p_n, lc * c_n)
    fs_i = jnp.transpose(ca_i[1:lc + 1], (1, 3, 0, 2)).reshape(g_n, p_n, lc * c_n)
    from_state = jnp.concatenate([fs_r, -fs_i], axis=1)
    n_steps = max(1, (n_blocks - 1).bit_length())
    rows = []
    ar, ai = pw_r[lc], pw_i[lc]
    for _ in range(n_steps):
        rows.append(jnp.stack([jnp.concatenate([ar, ar], -1), jnp.concatenate([-ai, ai], -1)], axis=1))
        ar, ai = _cmul(ar, ai, ar, ai)
    apow = jnp.stack(rows, axis=1)
    return intra.astype(BF16), to_state.astype(BF16), from_state.astype(BF16), apow


def _ssm_body(n_blocks, u_ref, intra_ref, ts_ref, fs_ref, apow_ref, o_ref):
    u = u_ref[0]
    y = jnp.dot(u, intra_ref[0], preferred_element_type=F32)
    hs = jnp.dot(u, ts_ref[0], preferred_element_type=F32)
    rows = hs.shape[0]
    half = hs.shape[1] // 2
    blk = lax.broadcasted_iota(jnp.int32, (rows, 1), 0) % n_blocks
    d = 1
    k = 0
    while d < n_blocks:
        sh = jnp.where(blk >= d, pltpu.roll(hs, d, axis=0), 0.0)
        hs = hs + apow_ref[0, k, 0:1, :] * sh + apow_ref[0, k, 1:2, :] * pltpu.roll(sh, half, axis=1)
        d *= 2
        k += 1
    h_in = jnp.where(blk >= 1, pltpu.roll(hs, 1, axis=0), 0.0)
    y = y + jnp.dot(h_in.astype(BF16), fs_ref[0], preferred_element_type=F32)
    o_ref[0] = _gelu(y).astype(BF16)


def _ssm_scan(u_g, intra, to_state, from_state, apow, n_blocks):
    g_n, rows, width = u_g.shape
    p2 = to_state.shape[2]
    n_steps = apow.shape[1]
    return pl.pallas_call(
        functools.partial(_ssm_body, n_blocks),
        grid=(g_n,),
        in_specs=[
            pl.BlockSpec((1, rows, width), lambda g: (g, 0, 0)),
            pl.BlockSpec((1, width, width), lambda g: (g, 0, 0)),
            pl.BlockSpec((1, width, p2), lambda g: (g, 0, 0)),
            pl.BlockSpec((1, p2, width), lambda g: (g, 0, 0)),
            pl.BlockSpec((1, n_steps, 2, p2), lambda g: (g, 0, 0, 0)),
        ],
        out_specs=pl.BlockSpec((1, rows, width), lambda g: (g, 0, 0)),
        out_shape=jax.ShapeDtypeStruct((g_n, rows, width), BF16),
        compiler_params=_params(("arbitrary",)),
        name="ssm_scan",
    )(u_g, intra, to_state, from_state, apow)


def _glu_body(g_ref, w_ref, o_ref):
    g = g_ref[...]
    z = jnp.dot(g, w_ref[...], preferred_element_type=F32)
    o_ref[...] = (g.astype(F32) * jax.nn.sigmoid(z)).astype(BF16)


def _glu(g, w):
    t, d = g.shape
    tm = min(ROW_TILE, t)
    return pl.pallas_call(
        _glu_body,
        grid=(t // tm,),
        in_specs=[pl.BlockSpec((tm, d), lambda i: (i, 0)), pl.BlockSpec((d, d), lambda i: (0, 0))],
        out_specs=pl.BlockSpec((tm, d), lambda i: (i, 0)),
        out_shape=jax.ShapeDtypeStruct((t, d), BF16),
        compiler_params=_params(("arbitrary",)),
        name="ssm_glu",
    )(g, w)


def _ssm_mixer(hb, bsz, seq, w_in, ops, w_glu):
    t, d = hb.shape
    lc = SSM_LC
    n_blocks = seq // lc
    g_n = d // SSM_GROUP_DIM
    u = _matmul(hb, w_in, BF16)
    u_g = u.reshape(bsz * n_blocks, lc, g_n, SSM_GROUP_DIM).transpose(2, 0, 1, 3).reshape(g_n, bsz * n_blocks, lc * SSM_GROUP_DIM)
    y_g = _ssm_scan(u_g, *ops, n_blocks)
    gl = y_g.reshape(g_n, bsz * n_blocks, lc, SSM_GROUP_DIM).transpose(1, 2, 0, 3).reshape(t, d)
    return _glu(gl, w_glu)


def _attn_tables(rel_bias, tq):
    rb = rel_bias.astype(F32)
    heads = rb.shape[0]
    period = 2 * tq - 1

    def toeplitz(offset):
        m = np.arange(period)
        diff = np.where(m < tq, -m, period - m) + offset
        diag = rb[:, np.clip(diff, -ATTN_MAX_REL, ATTN_MAX_REL) + ATTN_MAX_REL]
        rows = jnp.tile(diag, (1, tq))[:, :tq * (period - 1)].reshape(heads, tq, period - 1)
        return rows[:, :, :tq]

    qc = np.arange(tq)[:, None] // CHUNK
    kc = np.arange(tq)[None, :] // CHUNK
    cur = jnp.where((kc <= qc)[None], toeplitz(0), NEG_INF)
    prev = jnp.where((kc >= qc)[None], toeplitz(tq), NEG_INF)
    return prev, cur


def _attn_body(scale, q_ref, kp_ref, kc_ref, vp_ref, vc_ref, bp_ref, bc_ref, o_ref):
    i = pl.program_id(2)
    q = q_ref[...]
    nt = (((1,), (1,)), ((), ()))
    s_p = lax.dot_general(q, kp_ref[...], nt, preferred_element_type=F32) * scale + bp_ref[0]
    s_c = lax.dot_general(q, kc_ref[...], nt, preferred_element_type=F32) * scale + bc_ref[0]
    s_p = jnp.where(i == 0, NEG_INF, s_p)
    m = jnp.maximum(jnp.max(s_p, axis=-1, keepdims=True), jnp.max(s_c, axis=-1, keepdims=True))
    p_p = jnp.exp(s_p - m)
    p_c = jnp.exp(s_c - m)
    l = jnp.sum(p_p, axis=-1, keepdims=True) + jnp.sum(p_c, axis=-1, keepdims=True)
    o = jnp.dot(p_p.astype(BF16), vp_ref[...], preferred_element_type=F32)
    o = o + jnp.dot(p_c.astype(BF16), vc_ref[...], preferred_element_type=F32)
    o_ref[...] = (o / l).astype(BF16)


def _chunk_attention(qkv, bsz, seq, heads, bias_prev, bias_cur):
    t = qkv.shape[0]
    hd = qkv.shape[1] // (3 * heads)
    tq = ATTN_TQ
    nt = seq // tq
    cur = lambda h, b, i: (b * nt + i, 0)
    prev = lambda h, b, i: (b * nt + jnp.maximum(i - 1, 0), 0)

    def col(base, rows):
        return lambda h, b, i: (rows(h, b, i)[0], base + h)

    return pl.pallas_call(
        functools.partial(_attn_body, hd ** -0.5),
        grid=(heads, bsz, nt),
        in_specs=[
            pl.BlockSpec((tq, hd), col(0, cur)),
            pl.BlockSpec((tq, hd), col(heads, prev)),
            pl.BlockSpec((tq, hd), col(heads, cur)),
            pl.BlockSpec((tq, hd), col(2 * heads, prev)),
            pl.BlockSpec((tq, hd), col(2 * heads, cur)),
            pl.BlockSpec((1, tq, tq), lambda h, b, i: (h, 0, 0)),
            pl.BlockSpec((1, tq, tq), lambda h, b, i: (h, 0, 0)),
        ],
        out_specs=pl.BlockSpec((tq, hd), lambda h, b, i: (b * nt + i, h)),
        out_shape=jax.ShapeDtypeStruct((t, heads * hd), BF16),
        compiler_params=_params(("arbitrary", "arbitrary", "arbitrary")),
        name="chunk_attention",
    )(qkv, qkv, qkv, qkv, qkv, bias_prev, bias_cur)


def _sort_pairs(n):
    pairs = []
    p = 1
    while p < n:
        k = p
        while k >= 1:
            for j in range(k % p, n - k, 2 * k):
                for i in range(min(k, n - j - k)):
                    if (i + j) // (2 * p) == (i + j + k) // (2 * p):
                        pairs.append((i + j, i + j + k))
            k //= 2
        p *= 2
    return pairs


def _cmp_exchange(x, y):
    if x is None:
        return y, None
    if y is None:
        return x, None
    return jnp.maximum(x, y), jnp.minimum(x, y)


def _opt_max(x, y):
    if x is None:
        return y
    if y is None:
        return x
    return jnp.maximum(x, y)


def _merge_top(a, a_next, b, b_next):
    k = len(a)
    hi, lo_max = [], None
    for i in range(k):
        h, l = _cmp_exchange(a[i], b[k - 1 - i])
        hi.append(h)
        lo_max = _opt_max(lo_max, l)
    d = k // 2
    while d >= 1:
        for i in range(k):
            if i & d == 0:
                hi[i], hi[i + d] = _cmp_exchange(hi[i], hi[i + d])
        d //= 2
    return hi, _opt_max(_opt_max(lo_max, a_next), b_next)


def _top_sorted(vals, k):
    vals = list(vals) + [None] * (-len(vals) % k)
    pairs = _sort_pairs(k)
    groups = []
    for g in range(0, len(vals), k):
        v = vals[g:g + k]
        for i, j in pairs:
            v[i], v[j] = _cmp_exchange(v[i], v[j])
        groups.append((v, None))
    while len(groups) > 1:
        merged = [_merge_top(*groups[i], *groups[i + 1]) for i in range(0, len(groups) - 1, 2)]
        groups = merged + groups[len(groups) - len(groups) % 2:]
    return groups[0]


def _peer_score_body(q_ref, keys_ref, e2_ref, f1_ref, thr_ref, km1_scr, km2_scr, stat_scr):
    k = PEER_TOPK
    nk = keys_ref.shape[1]
    half = keys_ref.shape[2]
    ts = q_ref.shape[1]
    groups = ts // LANES
    assert groups == 8, "one sublane per 128-token group"

    def head(h, carry):
        base = pl.multiple_of(h * 2 * half, 2 * half)
        s1 = jnp.dot(keys_ref[0], q_ref[pl.ds(base, half), :], preferred_element_type=F32)
        s2 = jnp.dot(keys_ref[1], q_ref[pl.ds(base + half, half), :], preferred_element_type=F32)
        for g in range(groups):
            km1_scr[g * nk:(g + 1) * nk, :] = s1[:, g * LANES:(g + 1) * LANES]
            km2_scr[g * nk:(g + 1) * nk, :] = s2[:, g * LANES:(g + 1) * LANES]
        v1 = [km1_scr[pl.ds(key, groups, stride=nk), :] for key in range(nk)]
        v2 = [km2_scr[pl.ds(key, groups, stride=nk), :] for key in range(nk)]
        a, a_next = _top_sorted(v1, k)
        b, b_next = _top_sorted(v2, k)
        cand = [a[i] + b[j] for i in range(k) for j in range(k) if (i + 1) * (j + 1) <= k]
        cand += [a_next + b[0], a[0] + b_next]
        best, nxt = _top_sorted(cand, k)
        z = 1.0 + jnp.exp(best[1] - best[0])
        for r in range(2, k):
            z = z + jnp.exp(best[r] - best[0])
        stat_scr[0 * groups:1 * groups, :] = a[0]
        stat_scr[1 * groups:2 * groups, :] = b[0]
        stat_scr[2 * groups:3 * groups, :] = z
        stat_scr[3 * groups:4 * groups, :] = 0.5 * (best[k - 1] + nxt)
        for g in range(groups):
            cols = slice(g * LANES, (g + 1) * LANES)
            m1 = stat_scr[0 * groups + g:0 * groups + g + 1, :]
            m2 = stat_scr[1 * groups + g:1 * groups + g + 1, :]
            zg = stat_scr[2 * groups + g:2 * groups + g + 1, :]
            tau = stat_scr[3 * groups + g:3 * groups + g + 1, :]
            s1g = km1_scr[g * nk:(g + 1) * nk, :]
            s2g = km2_scr[g * nk:(g + 1) * nk, :]
            e2_ref[h, :, cols] = jnp.exp(s2g - m2).astype(BF16)
            f1_ref[h, :, cols] = jnp.exp(s1g - m1) / zg
            thr_ref[h, :, cols] = jnp.exp(tau - s1g - m2)
        return carry

    lax.fori_loop(0, PEER_HEADS, head, 0)


def _peer_scores(qt, keys):
    qd, t = qt.shape
    ts = PEER_TS
    nk = keys.shape[1]
    groups = ts // LANES
    spec = pl.BlockSpec((PEER_HEADS, nk, ts), lambda i: (0, 0, i))
    shape = jax.ShapeDtypeStruct((PEER_HEADS, nk, t), F32)
    return pl.pallas_call(
        _peer_score_body,
        grid=(t // ts,),
        in_specs=[
            pl.BlockSpec((qd, ts), lambda i: (0, i)),
            pl.BlockSpec((2, nk, PEER_HALF), lambda i: (0, 0, 0)),
        ],
        out_specs=[spec, spec, spec],
        out_shape=[jax.ShapeDtypeStruct(shape.shape, BF16), shape, shape],
        scratch_shapes=[
            pltpu.VMEM((groups * nk, LANES), F32),
            pltpu.VMEM((groups * nk, LANES), F32),
            pltpu.VMEM((4 * groups, LANES), F32),
        ],
        compiler_params=_params(("arbitrary",)),
        name="peer_scores",
    )(qt, keys)


def _peer_stage_rows(thr_ref, f1_ref, rows_ref, base, blk, n_rows):
    last_row = thr_ref.shape[1] - 1
    for jj in range(n_rows):
        row = jnp.clip(blk * n_rows + jj, 0, last_row)
        for h in range(PEER_HEADS):
            at = base + (jj * 2) * PEER_HEADS + h
            rows_ref[at:at + 1, :] = thr_ref[h, pl.ds(row, 1), :]
            rows_ref[at + PEER_HEADS:at + PEER_HEADS + 1, :] = f1_ref[h, pl.ds(row, 1), :]


def _peer_gate_piece(e2_ref, rows_ref, base, st_ref, wp_ref, jj, ls):
    nk = e2_ref.shape[1]
    rc_n = PEER_RC
    width = ls.stop - ls.start
    at = base + (jj * 2) * PEER_HEADS
    thr = [jnp.broadcast_to(rows_ref[at + h:at + h + 1, ls], (rc_n, width)).astype(BF16) for h in range(PEER_HEADS)]
    f1 = [jnp.broadcast_to(rows_ref[at + PEER_HEADS + h:at + PEER_HEADS + h + 1, ls], (rc_n, width)).astype(BF16)
          for h in range(PEER_HEADS)]
    zero = jnp.zeros((rc_n, width), BF16)
    for rc in range(nk // rc_n):
        rs = slice(rc * rc_n, (rc + 1) * rc_n)
        wg = None
        for h in range(PEER_HEADS):
            e = e2_ref[h, rs, ls]
            part = jnp.where(e >= thr[h], e, zero) * f1[h]
            wg = part if wg is None else wg + part
        es = slice(jj * nk + rc * rc_n, jj * nk + (rc + 1) * rc_n)
        wp_ref[es, ls] = wg * _gelu(st_ref[es, ls]).astype(BF16)


def _peer_half_step(xb_ref, u_ref, vt_ref, cols, e2_ref, rows_ref, base,
                    st_out_ref, st_in_ref, wp_out_ref, wp_in_ref, acc_ref):
    nk = e2_ref.shape[1]
    n_rows = st_in_ref.shape[0] // nk
    width = PEER_GATE_LANES
    n_pc = st_in_ref.shape[1] // width
    rows_per = acc_ref.shape[0] // n_pc
    contract_last = (((1,), (1,)), ((), ()))
    for pc in range(n_pc):
        ls = slice(pc * width, (pc + 1) * width)
        st_out_ref[:, ls] = lax.dot_general(u_ref[0, cols, :], xb_ref[ls, :], contract_last,
                                            preferred_element_type=F32)
        for jj in range(n_rows):
            _peer_gate_piece(e2_ref, rows_ref, base, st_in_ref, wp_out_ref, jj, ls)
        dr = slice(pc * rows_per, (pc + 1) * rows_per)
        acc_ref[dr, :] += jnp.dot(vt_ref[0, dr, cols], wp_in_ref[...], preferred_element_type=F32)


def _peer_dense_body(alpha, xb_ref, u_ref, vt_ref, e2_ref, f1_ref, thr_ref, res_ref, g_ref, b_ref,
                     h_ref, hb_ref, acc_ref, st0_ref, st1_ref, wp0_ref, wp1_ref, rows_ref):
    j = pl.program_id(1)
    nk = e2_ref.shape[1]
    half = u_ref.shape[1] // 2
    n_rows = half // nk

    @pl.when(j == 0)
    def _():
        acc_ref[...] = jnp.zeros(acc_ref.shape, F32)
        st1_ref[...] = jnp.zeros(st1_ref.shape, F32)
        wp0_ref[...] = jnp.zeros(wp0_ref.shape, BF16)

    per_half = n_rows * 2 * PEER_HEADS
    _peer_stage_rows(thr_ref, f1_ref, rows_ref, 0, 2 * j - 1, n_rows)
    _peer_stage_rows(thr_ref, f1_ref, rows_ref, per_half, 2 * j, n_rows)

    _peer_half_step(xb_ref, u_ref, vt_ref, slice(0, half), e2_ref, rows_ref, 0,
                    st0_ref, st1_ref, wp1_ref, wp0_ref, acc_ref)
    _peer_half_step(xb_ref, u_ref, vt_ref, slice(half, 2 * half), e2_ref, rows_ref, per_half,
                    st1_ref, st0_ref, wp0_ref, wp1_ref, acc_ref)

    @pl.when(j == pl.num_programs(1) - 1)
    def _():
        y = _layer_norm(alpha * res_ref[...] + acc_ref[...].T, g_ref[...], b_ref[...])
        h_ref[...] = y
        hb_ref[...] = y.astype(BF16)


def _peer_dense(xb, u, vt, e2, f1, thr, res, g, b, alpha):
    t, d = xb.shape
    nj, eb, _ = u.shape
    nk = e2.shape[1]
    tt = min(PEER_TT, t)
    hds = PEER_HEADS
    once = pl.Buffered(1)
    return pl.pallas_call(
        functools.partial(_peer_dense_body, alpha),
        grid=(t // tt, nj + 1),
        in_specs=[
            pl.BlockSpec((tt, d), lambda i, j: (i, 0)),
            pl.BlockSpec((1, eb, d), lambda i, j: (jnp.minimum(j, nj - 1), 0, 0)),
            pl.BlockSpec((1, d, eb), lambda i, j: (jnp.maximum(j - 1, 0), 0, 0)),
            pl.BlockSpec((hds, nk, tt), lambda i, j: (0, 0, i)),
            pl.BlockSpec((hds, nk, tt), lambda i, j: (0, 0, i), pipeline_mode=once),
            pl.BlockSpec((hds, nk, tt), lambda i, j: (0, 0, i), pipeline_mode=once),
            pl.BlockSpec((tt, d), lambda i, j: (i, 0), pipeline_mode=once),
            pl.BlockSpec((1, d), lambda i, j: (0, 0)),
            pl.BlockSpec((1, d), lambda i, j: (0, 0)),
        ],
        out_specs=[pl.BlockSpec((tt, d), lambda i, j: (i, 0)), pl.BlockSpec((tt, d), lambda i, j: (i, 0))],
        out_shape=[jax.ShapeDtypeStruct((t, d), F32), jax.ShapeDtypeStruct((t, d), BF16)],
        scratch_shapes=[
            pltpu.VMEM((d, tt), F32),
            pltpu.VMEM((eb // 2, tt), F32),
            pltpu.VMEM((eb // 2, tt), F32),
            pltpu.VMEM((eb // 2, tt), BF16),
            pltpu.VMEM((eb // 2, tt), BF16),
            pltpu.VMEM((2 * (eb // 2 // nk) * 2 * hds, tt), F32),
        ],
        compiler_params=_params(("arbitrary", "arbitrary")),
        name="peer_dense",
    )(xb, u, vt, e2, f1, thr, res, g.reshape(1, d), b.reshape(1, d))


def _expert_blocks(table, transpose):
    ne, d = table.shape
    blocks = table.astype(BF16).reshape(ne // PEER_EB, PEER_EB, d)
    return blocks.transpose(0, 2, 1) if transpose else blocks


def kernel(x, p, ln_mix_g, ln_mix_b, ln_ffn_g, ln_ffn_b, pool_w_in, pool_w_group, pool_scale, pool_w_out, ssm_w_in, ssm_lambda_re, ssm_lambda_im, ssm_log_step, ssm_b_re, ssm_b_im, ssm_c_re, ssm_c_im, ssm_d, ssm_w_glu, ssm_w_out, attn_w_qkv, attn_rel_bias, attn_w_out, peer_w_q, peer_sub_keys, peer_u, peer_v, ple_w_proj, ple_w_gate, ple_b_gate):
    bsz, seq, d = x.shape
    depth = p.shape[0]
    t = bsz * seq
    alpha = (2.0 * depth) ** 0.25
    h = x.reshape(t, d)
    hb = h.astype(BF16)
    for i in range(depth):
        kind, j = i % N_MIXERS, i // N_MIXERS
        if kind == 0:
            mixed = _pool_mixer(hb, seq, pool_w_in[j].astype(BF16), pool_w_group[j].astype(BF16), pool_scale[j])
            w_out = pool_w_out[j]
        elif kind == 1:
            ops = _ssm_operators(ssm_lambda_re[j], ssm_lambda_im[j], ssm_log_step[j], ssm_b_re[j], ssm_b_im[j],
                                 ssm_c_re[j], ssm_c_im[j], ssm_d[j], SSM_LC, seq // SSM_LC)
            mixed = _ssm_mixer(hb, bsz, seq, ssm_w_in[j].astype(BF16), ops, ssm_w_glu[j].astype(BF16))
            w_out = ssm_w_out[j]
        else:
            qkv = _matmul(hb, attn_w_qkv[j].astype(BF16), BF16)
            bias_prev, bias_cur = _attn_tables(attn_rel_bias[j], ATTN_TQ)
            mixed = _chunk_attention(qkv, bsz, seq, ATTN_HEADS, bias_prev, bias_cur)
            w_out = attn_w_out[j]
        h, hb, ht = _matmul_ln(mixed, w_out.astype(BF16), h, ln_mix_g[i], ln_mix_b[i], alpha)
        qt = _matmul(peer_w_q[i].T.astype(BF16), ht, BF16)
        e2, f1, thr = _peer_scores(qt, peer_sub_keys[i].astype(BF16))
        h, hb = _peer_dense(hb, _expert_blocks(peer_u[i], False), _expert_blocks(peer_v[i], True), e2, f1, thr,
                            h, ln_ffn_g[i], ln_ffn_b[i], alpha)
        h, hb = _ple(hb, h, p[i].reshape(t, -1).astype(BF16), ple_w_gate[i].astype(BF16),
                     ple_w_proj[i].astype(BF16), ple_b_gate[i])
    return h.reshape(bsz, seq, d)
```

```python
import functools
import math

import jax
import jax.numpy as jnp
import numpy as np
from jax import lax
from jax.experimental import pallas as pl
from jax.experimental.pallas import tpu as pltpu

F32 = jnp.float32
BF16 = jnp.bfloat16

CHUNK = 64
N_MIXERS = 3
LN_EPS = 1e-5
POOL_WINDOWS = (2, 4, 8, 16)
POOL_MAX_WINDOW = max(POOL_WINDOWS)
SSM_GROUP_DIM = 16
SSM_STATE = 64
ATTN_HEADS = 16
ATTN_LEFT_CHUNKS = 8
ATTN_MAX_REL = 2 * CHUNK
NEG_INF = -1e30
PEER_N_KEYS = 128
PEER_HEADS = 8
PEER_HALF = 128
PEER_TOPK = 16

LANES = 128
VMEM_LIMIT = 56 * 1024 * 1024
ROW_TILE = 256
MM_TM, MM_TN = 512, 1024
ATTN_TQ = ATTN_LEFT_CHUNKS * CHUNK
ATTN_ROWS = LANES
PEER_TS = 8 * LANES
PEER_TT = 512
PEER_EB = 512
PEER_RC = 8
PEER_GATE_LANES = 2 * LANES
SSM_LC = 32


def _params(sem):
    return pltpu.CompilerParams(dimension_semantics=sem, vmem_limit_bytes=VMEM_LIMIT)


def _divisor_tile(n, preferred):
    best = LANES
    for cand in range(LANES, min(n, preferred) + 1, LANES):
        if n % cand == 0:
            best = cand
    assert n % best == 0, (n, preferred)
    return best


def _gelu(x):
    return 0.5 * x * (1.0 + lax.erf(x * (1.0 / math.sqrt(2.0))))


def _layer_norm(z, g, b):
    mu = jnp.mean(z, axis=-1, keepdims=True)
    zc = z - mu
    var = jnp.mean(zc * zc, axis=-1, keepdims=True)
    return zc * lax.rsqrt(var + LN_EPS) * g + b


def _mm_body(a_ref, w_ref, o_ref):
    o_ref[...] = jnp.dot(a_ref[...], w_ref[...], preferred_element_type=F32).astype(o_ref.dtype)


def _matmul(a, w, out_dtype):
    m, k = a.shape
    n = w.shape[1]
    tm, tn = _divisor_tile(m, MM_TM), _divisor_tile(n, MM_TN)
    return pl.pallas_call(
        _mm_body,
        grid=(n // tn, m // tm),
        in_specs=[pl.BlockSpec((tm, k), lambda j, i: (i, 0)), pl.BlockSpec((k, tn), lambda j, i: (0, j))],
        out_specs=pl.BlockSpec((tm, tn), lambda j, i: (i, j)),
        out_shape=jax.ShapeDtypeStruct((m, n), out_dtype),
        compiler_params=_params(("arbitrary", "arbitrary")),
        name="matmul",
    )(a, w)


def _mm_ln_body(alpha, a_ref, w_ref, res_ref, g_ref, b_ref, h_ref, hb_ref, ht_ref):
    m = jnp.dot(a_ref[...], w_ref[...], preferred_element_type=F32)
    y = _layer_norm(alpha * res_ref[...] + m, g_ref[...], b_ref[...])
    h_ref[...] = y
    hb_ref[...] = y.astype(BF16)
    ht_ref[...] = y.T.astype(BF16)


def _matmul_ln(a, w, res, g, b, alpha):
    t, k = a.shape
    d = w.shape[1]
    tm = min(ROW_TILE, t)
    return pl.pallas_call(
        functools.partial(_mm_ln_body, alpha),
        grid=(t // tm,),
        in_specs=[
            pl.BlockSpec((tm, k), lambda i: (i, 0)),
            pl.BlockSpec((k, d), lambda i: (0, 0)),
            pl.BlockSpec((tm, d), lambda i: (i, 0)),
            pl.BlockSpec((1, d), lambda i: (0, 0)),
            pl.BlockSpec((1, d), lambda i: (0, 0)),
        ],
        out_specs=[pl.BlockSpec((tm, d), lambda i: (i, 0)), pl.BlockSpec((tm, d), lambda i: (i, 0)),
                   pl.BlockSpec((d, tm), lambda i: (0, i))],
        out_shape=[jax.ShapeDtypeStruct((t, d), F32), jax.ShapeDtypeStruct((t, d), BF16),
                   jax.ShapeDtypeStruct((d, t), BF16)],
        compiler_params=_params(("arbitrary",)),
        name="outproj_ln",
    )(a, w, res, g.reshape(1, d), b.reshape(1, d))


def _ple_body(hb_ref, h_ref, p_ref, wg_ref, wp_ref, bg_ref, o_ref, ob_ref):
    gate = jax.nn.sigmoid(jnp.dot(hb_ref[...], wg_ref[...], preferred_element_type=F32) + bg_ref[...])
    proj = jnp.dot(p_ref[...], wp_ref[...], preferred_element_type=F32)
    y = h_ref[...] + gate * proj
    o_ref[...] = y
    ob_ref[...] = y.astype(BF16)


def _ple(hb, h, p, wg, wp, bg):
    t, d = h.shape
    pd = p.shape[1]
    tm = min(ROW_TILE, t)
    return pl.pallas_call(
        _ple_body,
        grid=(t // tm,),
        in_specs=[
            pl.BlockSpec((tm, d), lambda i: (i, 0)),
            pl.BlockSpec((tm, d), lambda i: (i, 0)),
            pl.BlockSpec((tm, pd), lambda i: (i, 0)),
            pl.BlockSpec((d, d), lambda i: (0, 0)),
            pl.BlockSpec((pd, d), lambda i: (0, 0)),
            pl.BlockSpec((1, d), lambda i: (0, 0)),
        ],
        out_specs=[pl.BlockSpec((tm, d), lambda i: (i, 0)), pl.BlockSpec((tm, d), lambda i: (i, 0))],
        out_shape=[jax.ShapeDtypeStruct((t, d), F32), jax.ShapeDtypeStruct((t, d), BF16)],
        compiler_params=_params(("arbitrary",)),
        name="ple_gate",
    )(hb, h, p, wg, wp, bg.reshape(1, d))


def _pool_body(tiles_per_seq, x_ref, win_ref, wg_ref, sc_ref, o_ref, ext_ref):
    wmax = POOL_MAX_WINDOW
    tm = x_ref.shape[0]
    gd = wg_ref.shape[1]
    i = pl.program_id(0)

    @pl.when(i % tiles_per_seq == 0)
    def _():
        ext_ref[0:wmax, :] = jnp.zeros((wmax, ext_ref.shape[1]), F32)

    ext_ref[wmax:wmax + tm, :] = jnp.dot(x_ref[...], win_ref[...], preferred_element_type=F32)
    pos = (i % tiles_per_seq) * tm + lax.broadcasted_iota(jnp.int32, (tm, 1), 0)
    for g, w in enumerate(POOL_WINDOWS):
        cols = slice(g * gd, (g + 1) * gd)
        u = ext_ref[wmax:wmax + tm, cols]
        acc = u
        for k in range(1, w):
            acc = acc + ext_ref[wmax - k:wmax - k + tm, cols]
        count = jnp.minimum(pos + 1, w).astype(F32)
        pooled = acc / count - u
        mixed = jnp.dot(pooled.astype(BF16), wg_ref[g], preferred_element_type=F32) * sc_ref[:, cols]
        o_ref[:, cols] = mixed.astype(BF16)
    ext_ref[0:wmax, :] = ext_ref[tm:tm + wmax, :]


def _pool_mixer(hb, seq, w_in, w_group, scale):
    t, d = hb.shape
    tm = min(ROW_TILE, seq)
    ng, gd = w_group.shape[0], w_group.shape[1]
    return pl.pallas_call(
        functools.partial(_pool_body, seq // tm),
        grid=(t // tm,),
        in_specs=[
            pl.BlockSpec((tm, d), lambda i: (i, 0)),
            pl.BlockSpec((d, d), lambda i: (0, 0)),
            pl.BlockSpec((ng, gd, gd), lambda i: (0, 0, 0)),
            pl.BlockSpec((1, d), lambda i: (0, 0)),
        ],
        out_specs=pl.BlockSpec((tm, d), lambda i: (i, 0)),
        out_shape=jax.ShapeDtypeStruct((t, d), BF16),
        scratch_shapes=[pltpu.VMEM((tm + POOL_MAX_WINDOW, d), F32)],
        compiler_params=_params(("arbitrary",)),
        name="pool_mixer",
    )(hb, w_in, w_group, scale.reshape(1, d))


def _cmul(ar, ai, br, bi):
    return ar * br - ai * bi, ar * bi + ai * br


def _ssm_operators(lam_re, lam_im, log_step, b_re, b_im, c_re, c_im, d_skip, lc, n_blocks):
    f32 = F32
    lr, li = lam_re.astype(f32), lam_im.astype(f32)
    dt = jnp.exp(log_step.astype(f32))[:, None]
    mag = jnp.exp(dt * lr)
    abar_r, abar_i = mag * jnp.cos(dt * li), mag * jnp.sin(dt * li)
    den = lr * lr + li * li
    k_r = ((abar_r - 1.0) * lr + abar_i * li) / den
    k_i = (abar_i * lr - (abar_r - 1.0) * li) / den
    bb_r, bb_i = _cmul(k_r[..., None], k_i[..., None], b_re.astype(f32), b_im.astype(f32))
    cr, ci = c_re.astype(f32), c_im.astype(f32)
    g_n, p_n = lr.shape
    c_n = bb_r.shape[-1]

    def pw_step(carry, _):
        pr, pi = carry
        nr, ni = _cmul(pr, pi, abar_r, abar_i)
        return (nr, ni), (pr, pi)

    (_, _), (pw_r, pw_i) = lax.scan(pw_step, (jnp.ones_like(abar_r), jnp.zeros_like(abar_i)), None, length=lc + 1)
    ca_r = cr[None] * pw_r[:, :, None, :] - ci[None] * pw_i[:, :, None, :]
    ca_i = cr[None] * pw_i[:, :, None, :] + ci[None] * pw_r[:, :, None, :]
    kern = jnp.einsum('tgop,gpc->tgoc', ca_r[:lc], bb_r) - jnp.einsum('tgop,gpc->tgoc', ca_i[:lc], bb_i)
    eye = jnp.eye(c_n, dtype=f32)
    kern = kern.at[0].add(d_skip.astype(f32).reshape(g_n, c_n)[:, :, None] * eye[None])
    s_idx = jnp.arange(lc)[:, None]
    t_idx = jnp.arange(lc)[None, :]
    lag = t_idx - s_idx
    toep = kern[jnp.clip(lag, 0, lc - 1)] * (lag >= 0)[:, :, None, None, None].astype(f32)
    intra = jnp.transpose(toep, (2, 0, 4, 1, 3)).reshape(g_n, lc * c_n, lc * c_n)
    rev_r, rev_i = pw_r[lc - 1 - jnp.arange(lc)], pw_i[lc - 1 - jnp.arange(lc)]
    ts_r, ts_i = _cmul(rev_r[..., None], rev_i[..., None], bb_r[None], bb_i[None])
    to_state = jnp.concatenate([jnp.transpose(ts_r, (1, 0, 3, 2)), jnp.transpose(ts_i, (1, 0, 3, 2))], axis=-1)
    to_state = to_state.reshape(g_n, lc * c_n, 2 * p_n)
    fs_r = jnp.transpose(ca_r[1:lc + 1], (1, 3, 0, 2)).reshape(g_n, p_n, lc * c_n)
    fs_i = jnp.transpose(ca_i[1:lc + 1], (1, 3, 0, 2)).reshape(g_n, p_n, lc * c_n)
    from_state = jnp.concatenate([fs_r, -fs_i], axis=1)
    n_steps = max(1, (n_blocks - 1).bit_length())
    rows = []
    ar, ai = pw_r[lc], pw_i[lc]
    for _ in range(n_steps):
        rows.append(jnp.stack([jnp.concatenate([ar, ar], -1), jnp.concatenate([-ai, ai], -1)], axis=1))
        ar, ai = _cmul(ar, ai, ar, ai)
    apow = jnp.stack(rows, axis=1)
    return intra.astype(BF16), to_state.astype(BF16), from_state.astype(BF16), apow


def _ssm_body(n_blocks, u_ref, intra_ref, ts_ref, fs_ref, apow_ref, o_ref):
    u = u_ref[0]
    y = jnp.dot(u, intra_ref[0], preferred_element_type=F32)
    hs = jnp.dot(u, ts_ref[0], preferred_element_type=F32)
    rows = hs.shape[0]
    half = hs.shape[1] // 2
    blk = lax.broadcasted_iota(jnp.int32, (rows, 1), 0) % n_blocks
    d = 1
    k = 0
    while d < n_blocks:
        sh = jnp.where(blk >= d, pltpu.roll(hs, d, axis=0), 0.0)
        hs = hs + apow_ref[0, k, 0:1, :] * sh + apow_ref[0, k, 1:2, :] * pltpu.roll(sh, half, axis=1)
        d *= 2
        k += 1
    h_in = jnp.where(blk >= 1, pltpu.roll(hs, 1, axis=0), 0.0)
    y = y + jnp.dot(h_in.astype(BF16), fs_ref[0], preferred_element_type=F32)
    o_ref[0] = _gelu(y).astype(BF16)


def _ssm_scan(u_g, intra, to_state, from_state, apow, n_blocks):
    g_n, rows, width = u_g.shape
    p2 = to_state.shape[2]
    n_steps = apow.shape[1]
    return pl.pallas_call(
        functools.partial(_ssm_body, n_blocks),
        grid=(g_n,),
        in_specs=[
            pl.BlockSpec((1, rows, width), lambda g: (g, 0, 0)),
            pl.BlockSpec((1, width, width), lambda g: (g, 0, 0)),
            pl.BlockSpec((1, width, p2), lambda g: (g, 0, 0)),
            pl.BlockSpec((1, p2, width), lambda g: (g, 0, 0)),
            pl.BlockSpec((1, n_steps, 2, p2), lambda g: (g, 0, 0, 0)),
        ],
        out_specs=pl.BlockSpec((1, rows, width), lambda g: (g, 0, 0)),
        out_shape=jax.ShapeDtypeStruct((g_n, rows, width), BF16),
        compiler_params=_params(("arbitrary",)),
        name="ssm_scan",
    )(u_g, intra, to_state, from_state, apow)


def _glu_body(g_ref, w_ref, o_ref):
    g = g_ref[...]
    z = jnp.dot(g, w_ref[...], preferred_element_type=F32)
    o_ref[...] = (g.astype(F32) * jax.nn.sigmoid(z)).astype(BF16)


def _glu(g, w):
    t, d = g.shape
    tm = min(ROW_TILE, t)
    return pl.pallas_call(
        _glu_body,
        grid=(t // tm,),
        in_specs=[pl.BlockSpec((tm, d), lambda i: (i, 0)), pl.BlockSpec((d, d), lambda i: (0, 0))],
        out_specs=pl.BlockSpec((tm, d), lambda i: (i, 0)),
        out_shape=jax.ShapeDtypeStruct((t, d), BF16),
        compiler_params=_params(("arbitrary",)),
        name="ssm_glu",
    )(g, w)


def _ssm_mixer(hb, bsz, seq, w_in, ops, w_glu):
    t, d = hb.shape
    lc = SSM_LC
    n_blocks = seq // lc
    g_n = d // SSM_GROUP_DIM
    u = _matmul(hb, w_in, BF16)
    u_g = u.reshape(bsz * n_blocks, lc, g_n, SSM_GROUP_DIM).transpose(2, 0, 1, 3).reshape(g_n, bsz * n_blocks, lc * SSM_GROUP_DIM)
    y_g = _ssm_scan(u_g, *ops, n_blocks)
    gl = y_g.reshape(g_n, bsz * n_blocks, lc, SSM_GROUP_DIM).transpose(1, 2, 0, 3).reshape(t, d)
    return _glu(gl, w_glu)


def _attn_tables(rel_bias, tq):
    rb = rel_bias.astype(F32)
    heads = rb.shape[0]
    period = 2 * tq - 1

    def toeplitz(offset):
        m = np.arange(period)
        diff = np.where(m < tq, -m, period - m) + offset
        diag = rb[:, np.clip(diff, -ATTN_MAX_REL, ATTN_MAX_REL) + ATTN_MAX_REL]
        rows = jnp.tile(diag, (1, tq))[:, :tq * (period - 1)].reshape(heads, tq, period - 1)
        return rows[:, :, :tq]

    qc = np.arange(tq)[:, None] // CHUNK
    kc = np.arange(tq)[None, :] // CHUNK
    cur = jnp.where((kc <= qc)[None], toeplitz(0), NEG_INF)
    prev = jnp.where((kc >= qc)[None], toeplitz(tq), NEG_INF)
    return prev, cur


def _attn_body(scale, q_ref, kp_ref, kc_ref, vp_ref, vc_ref, bp_ref, bc_ref, o_ref, s_scr, p_scr, l_scr):
    i = pl.program_id(2)
    tq = q_ref.shape[0]
    rows_n = ATTN_ROWS
    width = tq + rows_n
    contract_last = (((1,), (1,)), ((), ()))
    for r in range(tq // rows_n):
        rows = slice(r * rows_n, (r + 1) * rows_n)
        lo, hi, pw = r * rows_n, (r + 1) * rows_n, tq - r * rows_n
        q = q_ref[rows, :]
        s_p = lax.dot_general(q, kp_ref[lo:, :], contract_last, preferred_element_type=F32) * scale + bp_ref[0, rows, lo:]
        s_scr[rows, 0:pw] = jnp.where(i == 0, NEG_INF, s_p)
        s_scr[rows, pw:width] = (lax.dot_general(q, kc_ref[:hi, :], contract_last, preferred_element_type=F32) * scale
                                 + bc_ref[0, rows, :hi])
    sub = 64
    for r in range(tq // sub):
        rows = slice(r * sub, (r + 1) * sub)
        s = s_scr[rows, :]
        p = jnp.exp(s - jnp.max(s, axis=-1, keepdims=True))
        l_scr[rows, :] = jnp.broadcast_to(jnp.sum(p, axis=-1, keepdims=True), (sub, LANES))
        p_scr[rows, :] = p.astype(BF16)
    for r in range(tq // rows_n):
        rows = slice(r * rows_n, (r + 1) * rows_n)
        lo, hi, pw = r * rows_n, (r + 1) * rows_n, tq - r * rows_n
        o = jnp.dot(p_scr[rows, 0:pw], vp_ref[lo:, :], preferred_element_type=F32)
        o = o + jnp.dot(p_scr[rows, pw:width], vc_ref[:hi, :], preferred_element_type=F32)
        o_ref[rows, :] = (o / l_scr[rows, :]).astype(BF16)


def _chunk_attention(qkv, bsz, seq, heads, bias_prev, bias_cur):
    t = qkv.shape[0]
    hd = qkv.shape[1] // (3 * heads)
    tq = ATTN_TQ
    nt = seq // tq
    cur = lambda h, b, i: (b * nt + i, 0)
    prev = lambda h, b, i: (b * nt + jnp.maximum(i - 1, 0), 0)

    def col(base, rows):
        return lambda h, b, i: (rows(h, b, i)[0], base + h)

    return pl.pallas_call(
        functools.partial(_attn_body, hd ** -0.5),
        grid=(heads, bsz, nt),
        in_specs=[
            pl.BlockSpec((tq, hd), col(0, cur)),
            pl.BlockSpec((tq, hd), col(heads, prev)),
            pl.BlockSpec((tq, hd), col(heads, cur)),
            pl.BlockSpec((tq, hd), col(2 * heads, prev)),
            pl.BlockSpec((tq, hd), col(2 * heads, cur)),
            pl.BlockSpec((1, tq, tq), lambda h, b, i: (h, 0, 0)),
            pl.BlockSpec((1, tq, tq), lambda h, b, i: (h, 0, 0)),
        ],
        out_specs=pl.BlockSpec((tq, hd), lambda h, b, i: (b * nt + i, h)),
        out_shape=jax.ShapeDtypeStruct((t, heads * hd), BF16),
        scratch_shapes=[
            pltpu.VMEM((tq, tq + ATTN_ROWS), F32),
            pltpu.VMEM((tq, tq + ATTN_ROWS), BF16),
            pltpu.VMEM((tq, LANES), F32),
        ],
        compiler_params=_params(("arbitrary", "arbitrary", "arbitrary")),
        name="chunk_attention",
    )(qkv, qkv, qkv, qkv, qkv, bias_prev, bias_cur)


def _sort_pairs(n):
    pairs = []
    p = 1
    while p < n:
        k = p
        while k >= 1:
            for j in range(k % p, n - k, 2 * k):
                for i in range(min(k, n - j - k)):
                    if (i + j) // (2 * p) == (i + j + k) // (2 * p):
                        pairs.append((i + j, i + j + k))
            k //= 2
        p *= 2
    return pairs


def _cmp_exchange(x, y):
    if x is None:
        return y, None
    if y is None:
        return x, None
    return jnp.maximum(x, y), jnp.minimum(x, y)


def _opt_max(x, y):
    if x is None:
        return y
    if y is None:
        return x
    return jnp.maximum(x, y)


def _merge_top(a, a_next, b, b_next):
    k = len(a)
    hi, lo_max = [], None
    for i in range(k):
        h, l = _cmp_exchange(a[i], b[k - 1 - i])
        hi.append(h)
        lo_max = _opt_max(lo_max, l)
    d = k // 2
    while d >= 1:
        for i in range(k):
            if i & d == 0:
                hi[i], hi[i + d] = _cmp_exchange(hi[i], hi[i + d])
        d //= 2
    return hi, _opt_max(_opt_max(lo_max, a_next), b_next)


def _top_sorted(vals, k):
    vals = list(vals) + [None] * (-len(vals) % k)
    pairs = _sort_pairs(k)
    groups = []
    for g in range(0, len(vals), k):
        v = vals[g:g + k]
        for i, j in pairs:
            v[i], v[j] = _cmp_exchange(v[i], v[j])
        groups.append((v, None))
    while len(groups) > 1:
        merged = [_merge_top(*groups[i], *groups[i + 1]) for i in range(0, len(groups) - 1, 2)]
        groups = merged + groups[len(groups) - len(groups) % 2:]
    return groups[0]


def _peer_score_body(q_ref, keys_ref, e2_ref, f1_ref, thr_ref, km1_scr, km2_scr, stat_scr):
    k = PEER_TOPK
    nk = keys_ref.shape[1]
    half = keys_ref.shape[2]
    ts = q_ref.shape[1]
    groups = ts // LANES
    assert groups == 8, "one sublane per 128-token group"

    def head(h, carry):
        base = pl.multiple_of(h * 2 * half, 2 * half)
        s1 = jnp.dot(keys_ref[0], q_ref[pl.ds(base, half), :], preferred_element_type=F32)
        s2 = jnp.dot(keys_ref[1], q_ref[pl.ds(base + half, half), :], preferred_element_type=F32)
        for g in range(groups):
            km1_scr[g * nk:(g + 1) * nk, :] = s1[:, g * LANES:(g + 1) * LANES]
            km2_scr[g * nk:(g + 1) * nk, :] = s2[:, g * LANES:(g + 1) * LANES]
        v1 = [km1_scr[pl.ds(key, groups, stride=nk), :] for key in range(nk)]
        v2 = [km2_scr[pl.ds(key, groups, stride=nk), :] for key in range(nk)]
        a, a_next = _top_sorted(v1, k)
        b, b_next = _top_sorted(v2, k)
        cand = [a[i] + b[j] for i in range(k) for j in range(k) if (i + 1) * (j + 1) <= k]
        cand += [a_next + b[0], a[0] + b_next]
        best, nxt = _top_sorted(cand, k)
        z = 1.0 + jnp.exp(best[1] - best[0])
        for r in range(2, k):
            z = z + jnp.exp(best[r] - best[0])
        stat_scr[0 * groups:1 * groups, :] = a[0]
        stat_scr[1 * groups:2 * groups, :] = b[0]
        stat_scr[2 * groups:3 * groups, :] = z
        stat_scr[3 * groups:4 * groups, :] = 0.5 * (best[k - 1] + nxt)
        for g in range(groups):
            cols = slice(g * LANES, (g + 1) * LANES)
            m1 = stat_scr[0 * groups + g:0 * groups + g + 1, :]
            m2 = stat_scr[1 * groups + g:1 * groups + g + 1, :]
            zg = stat_scr[2 * groups + g:2 * groups + g + 1, :]
            tau = stat_scr[3 * groups + g:3 * groups + g + 1, :]
            s1g = km1_scr[g * nk:(g + 1) * nk, :]
            s2g = km2_scr[g * nk:(g + 1) * nk, :]
            e2_ref[h, :, cols] = jnp.exp(s2g - m2).astype(BF16)
            f1_ref[h, :, cols] = jnp.exp(s1g - m1) / zg
            thr_ref[h, :, cols] = jnp.exp(tau - s1g - m2)
        return carry

    lax.fori_loop(0, PEER_HEADS, head, 0)


def _peer_scores(qt, keys):
    qd, t = qt.shape
    ts = PEER_TS
    nk = keys.shape[1]
    groups = ts // LANES
    spec = pl.BlockSpec((PEER_HEADS, nk, ts), lambda i: (0, 0, i))
    shape = jax.ShapeDtypeStruct((PEER_HEADS, nk, t), F32)
    return pl.pallas_call(
        _peer_score_body,
        grid=(t // ts,),
        in_specs=[
            pl.BlockSpec((qd, ts), lambda i: (0, i)),
            pl.BlockSpec((2, nk, PEER_HALF), lambda i: (0, 0, 0)),
        ],
        out_specs=[spec, spec, spec],
        out_shape=[jax.ShapeDtypeStruct(shape.shape, BF16), shape, shape],
        scratch_shapes=[
            pltpu.VMEM((groups * nk, LANES), F32),
            pltpu.VMEM((groups * nk, LANES), F32),
            pltpu.VMEM((4 * groups, LANES), F32),
        ],
        compiler_params=_params(("arbitrary",)),
        name="peer_scores",
    )(qt, keys)


def _peer_stage_rows(thr_ref, f1_ref, rows_ref, base, blk, n_rows):
    last_row = thr_ref.shape[1] - 1
    for jj in range(n_rows):
        row = jnp.clip(blk * n_rows + jj, 0, last_row)
        for h in range(PEER_HEADS):
            at = base + (jj * 2) * PEER_HEADS + h
            rows_ref[at:at + 1, :] = thr_ref[h, pl.ds(row, 1), :]
            rows_ref[at + PEER_HEADS:at + PEER_HEADS + 1, :] = f1_ref[h, pl.ds(row, 1), :]


def _peer_gate_piece(e2_ref, rows_ref, base, st_ref, wp_ref, jj, ls):
    nk = e2_ref.shape[1]
    rc_n = PEER_RC
    width = ls.stop - ls.start
    at = base + (jj * 2) * PEER_HEADS
    thr = [jnp.broadcast_to(rows_ref[at + h:at + h + 1, ls], (rc_n, width)).astype(BF16) for h in range(PEER_HEADS)]
    f1 = [jnp.broadcast_to(rows_ref[at + PEER_HEADS + h:at + PEER_HEADS + h + 1, ls], (rc_n, width)).astype(BF16)
          for h in range(PEER_HEADS)]
    zero = jnp.zeros((rc_n, width), BF16)
    for rc in range(nk // rc_n):
        rs = slice(rc * rc_n, (rc + 1) * rc_n)
        wg = None
        for h in range(PEER_HEADS):
            e = e2_ref[h, rs, ls]
            part = jnp.where(e >= thr[h], e, zero) * f1[h]
            wg = part if wg is None else wg + part
        es = slice(jj * nk + rc * rc_n, jj * nk + (rc + 1) * rc_n)
        wp_ref[es, ls] = wg * _gelu(st_ref[es, ls]).astype(BF16)


def _peer_half_step(xb_ref, u_ref, vt_ref, cols, e2_ref, rows_ref, base,
                    st_out_ref, st_in_ref, wp_out_ref, wp_in_ref, acc_ref):
    nk = e2_ref.shape[1]
    n_rows = st_in_ref.shape[0] // nk
    width = PEER_GATE_LANES
    n_pc = st_in_ref.shape[1] // width
    rows_per = acc_ref.shape[0] // n_pc
    contract_last = (((1,), (1,)), ((), ()))
    for pc in range(n_pc):
        ls = slice(pc * width, (pc + 1) * width)
        st_out_ref[:, ls] = lax.dot_general(u_ref[0, cols, :], xb_ref[ls, :], contract_last,
                                            preferred_element_type=F32)
        for jj in range(n_rows):
            _peer_gate_piece(e2_ref, rows_ref, base, st_in_ref, wp_out_ref, jj, ls)
        dr = slice(pc * rows_per, (pc + 1) * rows_per)
        acc_ref[dr, :] += jnp.dot(vt_ref[0, dr, cols], wp_in_ref[...], preferred_element_type=F32)


def _peer_dense_body(alpha, xb_ref, u_ref, vt_ref, e2_ref, f1_ref, thr_ref, res_ref, g_ref, b_ref,
                     h_ref, hb_ref, acc_ref, st0_ref, st1_ref, wp0_ref, wp1_ref, rows_ref):
    j = pl.program_id(1)
    nk = e2_ref.shape[1]
    half = u_ref.shape[1] // 2
    n_rows = half // nk

    @pl.when(j == 0)
    def _():
        acc_ref[...] = jnp.zeros(acc_ref.shape, F32)
        st1_ref[...] = jnp.zeros(st1_ref.shape, F32)
        wp0_ref[...] = jnp.zeros(wp0_ref.shape, BF16)

    per_half = n_rows * 2 * PEER_HEADS
    _peer_stage_rows(thr_ref, f1_ref, rows_ref, 0, 2 * j - 1, n_rows)
    _peer_stage_rows(thr_ref, f1_ref, rows_ref, per_half, 2 * j, n_rows)

    _peer_half_step(xb_ref, u_ref, vt_ref, slice(0, half), e2_ref, rows_ref, 0,
                    st0_ref, st1_ref, wp1_ref, wp0_ref, acc_ref)
    _peer_half_step(xb_ref, u_ref, vt_ref, slice(half, 2 * half), e2_ref, rows_ref, per_half,
                    st1_ref, st0_ref, wp0_ref, wp1_ref, acc_ref)

    @pl.when(j == pl.num_programs(1) - 1)
    def _():
        y = _layer_norm(alpha * res_ref[...] + acc_ref[...].T, g_ref[...], b_ref[...])
        h_ref[...] = y
        hb_ref[...] = y.astype(BF16)


def _peer_dense(xb, u, vt, e2, f1, thr, res, g, b, alpha):
    t, d = xb.shape
    nj, eb, _ = u.shape
    nk = e2.shape[1]
    tt = min(PEER_TT, t)
    hds = PEER_HEADS
    once = pl.Buffered(1)
    return pl.pallas_call(
        functools.partial(_peer_dense_body, alpha),
        grid=(t // tt, nj + 1),
        in_specs=[
            pl.BlockSpec((tt, d), lambda i, j: (i, 0)),
            pl.BlockSpec((1, eb, d), lambda i, j: (jnp.minimum(j, nj - 1), 0, 0)),
            pl.BlockSpec((1, d, eb), lambda i, j: (jnp.maximum(j - 1, 0), 0, 0)),
            pl.BlockSpec((hds, nk, tt), lambda i, j: (0, 0, i)),
            pl.BlockSpec((hds, nk, tt), lambda i, j: (0, 0, i), pipeline_mode=once),
            pl.BlockSpec((hds, nk, tt), lambda i, j: (0, 0, i), pipeline_mode=once),
            pl.BlockSpec((tt, d), lambda i, j: (i, 0), pipeline_mode=once),
            pl.BlockSpec((1, d), lambda i, j: (0, 0)),
            pl.BlockSpec((1, d), lambda i, j: (0, 0)),
        ],
        out_specs=[pl.BlockSpec((tt, d), lambda i, j: (i, 0)), pl.BlockSpec((tt, d), lambda i, j: (i, 0))],
        out_shape=[jax.ShapeDtypeStruct((t, d), F32), jax.ShapeDtypeStruct((t, d), BF16)],
        scratch_shapes=[
            pltpu.VMEM((d, tt), F32),
            pltpu.VMEM((eb // 2, tt), F32),
            pltpu.VMEM((eb // 2, tt), F32),
            pltpu.VMEM((eb // 2, tt), BF16),
            pltpu.VMEM((eb // 2, tt), BF16),
            pltpu.VMEM((2 * (eb // 2 // nk) * 2 * hds, tt), F32),
        ],
        compiler_params=_params(("arbitrary", "arbitrary")),
        name="peer_dense",
    )(xb, u, vt, e2, f1, thr, res, g.reshape(1, d), b.reshape(1, d))


def _expert_blocks(table, transpose):
    ne, d = table.shape
    blocks = table.astype(BF16).reshape(ne // PEER_EB, PEER_EB, d)
    return blocks.transpose(0, 2, 1) if transpose else blocks


def kernel(x, p, ln_mix_g, ln_mix_b, ln_ffn_g, ln_ffn_b, pool_w_in, pool_w_group, pool_scale, pool_w_out, ssm_w_in, ssm_lambda_re, ssm_lambda_im, ssm_log_step, ssm_b_re, ssm_b_im, ssm_c_re, ssm_c_im, ssm_d, ssm_w_glu, ssm_w_out, attn_w_qkv, attn_rel_bias, attn_w_out, peer_w_q, peer_sub_keys, peer_u, peer_v, ple_w_proj, ple_w_gate, ple_b_gate):
    bsz, seq, d = x.shape
    depth = p.shape[0]
    t = bsz * seq
    alpha = (2.0 * depth) ** 0.25
    h = x.reshape(t, d)
    hb = h.astype(BF16)
    for i in range(depth):
        kind, j = i % N_MIXERS, i // N_MIXERS
        if kind == 0:
            mixed = _pool_mixer(hb, seq, pool_w_in[j].astype(BF16), pool_w_group[j].astype(BF16), pool_scale[j])
            w_out = pool_w_out[j]
        elif kind == 1:
            ops = _ssm_operators(ssm_lambda_re[j], ssm_lambda_im[j], ssm_log_step[j], ssm_b_re[j], ssm_b_im[j],
                                 ssm_c_re[j], ssm_c_im[j], ssm_d[j], SSM_LC, seq // SSM_LC)
            mixed = _ssm_mixer(hb, bsz, seq, ssm_w_in[j].astype(BF16), ops, ssm_w_glu[j].astype(BF16))
            w_out = ssm_w_out[j]
        else:
            qkv = _matmul(hb, attn_w_qkv[j].astype(BF16), BF16)
            bias_prev, bias_cur = _attn_tables(attn_rel_bias[j], ATTN_TQ)
            mixed = _chunk_attention(qkv, bsz, seq, ATTN_HEADS, bias_prev, bias_cur)
            w_out = attn_w_out[j]
        h, hb, ht = _matmul_ln(mixed, w_out.astype(BF16), h, ln_mix_g[i], ln_mix_b[i], alpha)
        qt = _matmul(peer_w_q[i].T.astype(BF16), ht, BF16)
        e2, f1, thr = _peer_scores(qt, peer_sub_keys[i].astype(BF16))
        h, hb = _peer_dense(hb, _expert_blocks(peer_u[i], False), _expert_blocks(peer_v[i], True), e2, f1, thr,
                            h, ln_ffn_g[i], ln_ffn_b[i], alpha)
        h, hb = _ple(hb, h, p[i].reshape(t, -1).astype(BF16), ple_w_gate[i].astype(BF16),
                     ple_w_proj[i].astype(BF16), ple_b_gate[i])
    return h.reshape(bsz, seq, d)
```

```python
import functools
import math

import jax
import jax.numpy as jnp
import numpy as np
from jax import lax
from jax.experimental import pallas as pl
from jax.experimental.pallas import tpu as pltpu

F32 = jnp.float32
BF16 = jnp.bfloat16

CHUNK = 64
N_MIXERS = 3
LN_EPS = 1e-5
POOL_WINDOWS = (2, 4, 8, 16)
POOL_MAX_WINDOW = max(POOL_WINDOWS)
SSM_GROUP_DIM = 16
SSM_STATE = 64
ATTN_HEADS = 16
ATTN_LEFT_CHUNKS = 8
ATTN_MAX_REL = 2 * CHUNK
NEG_INF = -1e30
PEER_N_KEYS = 128
PEER_HEADS = 8
PEER_HALF = 128
PEER_TOPK = 16

LANES = 128
VMEM_LIMIT = 56 * 1024 * 1024
ROW_TILE = 512
MM_TM, MM_TN = 1024, 1024
ATTN_TQ = ATTN_LEFT_CHUNKS * CHUNK
ATTN_ROWS = LANES
PEER_TS = 8 * LANES
PEER_TT = 512
PEER_EB = 512
PEER_RC = 8
PEER_GATE_LANES = 2 * LANES
SSM_LC = 32


def _params(sem):
    return pltpu.CompilerParams(dimension_semantics=sem, vmem_limit_bytes=VMEM_LIMIT)


def _divisor_tile(n, preferred):
    best = LANES
    for cand in range(LANES, min(n, preferred) + 1, LANES):
        if n % cand == 0:
            best = cand
    assert n % best == 0, (n, preferred)
    return best


def _gelu(x):
    return 0.5 * x * (1.0 + lax.erf(x * (1.0 / math.sqrt(2.0))))


def _layer_norm(z, g, b):
    mu = jnp.mean(z, axis=-1, keepdims=True)
    zc = z - mu
    var = jnp.mean(zc * zc, axis=-1, keepdims=True)
    return zc * lax.rsqrt(var + LN_EPS) * g + b


def _mm_body(a_ref, w_ref, o_ref):
    o_ref[...] = jnp.dot(a_ref[...], w_ref[...], preferred_element_type=F32).astype(o_ref.dtype)


def _matmul(a, w, out_dtype):
    m, k = a.shape
    n = w.shape[1]
    tm, tn = _divisor_tile(m, MM_TM), _divisor_tile(n, MM_TN)
    return pl.pallas_call(
        _mm_body,
        grid=(n // tn, m // tm),
        in_specs=[pl.BlockSpec((tm, k), lambda j, i: (i, 0)), pl.BlockSpec((k, tn), lambda j, i: (0, j))],
        out_specs=pl.BlockSpec((tm, tn), lambda j, i: (i, j)),
        out_shape=jax.ShapeDtypeStruct((m, n), out_dtype),
        compiler_params=_params(("arbitrary", "arbitrary")),
        name="matmul",
    )(a, w)


def _mm_ln_body(alpha, a_ref, w_ref, res_ref, g_ref, b_ref, h_ref, hb_ref, ht_ref):
    m = jnp.dot(a_ref[...], w_ref[...], preferred_element_type=F32)
    y = _layer_norm(alpha * res_ref[...] + m, g_ref[...], b_ref[...])
    h_ref[...] = y
    hb_ref[...] = y.astype(BF16)
    ht_ref[...] = y.T.astype(BF16)


def _matmul_ln(a, w, res, g, b, alpha):
    t, k = a.shape
    d = w.shape[1]
    tm = min(ROW_TILE, t)
    return pl.pallas_call(
        functools.partial(_mm_ln_body, alpha),
        grid=(t // tm,),
        in_specs=[
            pl.BlockSpec((tm, k), lambda i: (i, 0)),
            pl.BlockSpec((k, d), lambda i: (0, 0)),
            pl.BlockSpec((tm, d), lambda i: (i, 0)),
            pl.BlockSpec((1, d), lambda i: (0, 0)),
            pl.BlockSpec((1, d), lambda i: (0, 0)),
        ],
        out_specs=[pl.BlockSpec((tm, d), lambda i: (i, 0)), pl.BlockSpec((tm, d), lambda i: (i, 0)),
                   pl.BlockSpec((d, tm), lambda i: (0, i))],
        out_shape=[jax.ShapeDtypeStruct((t, d), F32), jax.ShapeDtypeStruct((t, d), BF16),
                   jax.ShapeDtypeStruct((d, t), BF16)],
        compiler_params=_params(("arbitrary",)),
        name="outproj_ln",
    )(a, w, res, g.reshape(1, d), b.reshape(1, d))


def _ple_body(hb_ref, h_ref, p_ref, wg_ref, wp_ref, bg_ref, o_ref, ob_ref):
    gate = jax.nn.sigmoid(jnp.dot(hb_ref[...], wg_ref[...], preferred_element_type=F32) + bg_ref[...])
    proj = jnp.dot(p_ref[...], wp_ref[...], preferred_element_type=F32)
    y = h_ref[...] + gate * proj
    o_ref[...] = y
    ob_ref[...] = y.astype(BF16)


def _ple(hb, h, p, wg, wp, bg):
    t, d = h.shape
    pd = p.shape[1]
    tm = min(ROW_TILE, t)
    return pl.pallas_call(
        _ple_body,
        grid=(t // tm,),
        in_specs=[
            pl.BlockSpec((tm, d), lambda i: (i, 0)),
            pl.BlockSpec((tm, d), lambda i: (i, 0)),
            pl.BlockSpec((tm, pd), lambda i: (i, 0)),
            pl.BlockSpec((d, d), lambda i: (0, 0)),
            pl.BlockSpec((pd, d), lambda i: (0, 0)),
            pl.BlockSpec((1, d), lambda i: (0, 0)),
        ],
        out_specs=[pl.BlockSpec((tm, d), lambda i: (i, 0)), pl.BlockSpec((tm, d), lambda i: (i, 0))],
        out_shape=[jax.ShapeDtypeStruct((t, d), F32), jax.ShapeDtypeStruct((t, d), BF16)],
        compiler_params=_params(("arbitrary",)),
        name="ple_gate",
    )(hb, h, p, wg, wp, bg.reshape(1, d))


def _pool_body(tiles_per_seq, x_ref, win_ref, wg_ref, sc_ref, o_ref, ext_ref):
    wmax = POOL_MAX_WINDOW
    tm = x_ref.shape[0]
    gd = wg_ref.shape[1]
    i = pl.program_id(0)

    @pl.when(i % tiles_per_seq == 0)
    def _():
        ext_ref[0:wmax, :] = jnp.zeros((wmax, ext_ref.shape[1]), F32)

    ext_ref[wmax:wmax + tm, :] = jnp.dot(x_ref[...], win_ref[...], preferred_element_type=F32)
    pos = (i % tiles_per_seq) * tm + lax.broadcasted_iota(jnp.int32, (tm, 1), 0)
    for g, w in enumerate(POOL_WINDOWS):
        cols = slice(g * gd, (g + 1) * gd)
        u = ext_ref[wmax:wmax + tm, cols]
        acc = u
        for k in range(1, w):
            acc = acc + ext_ref[wmax - k:wmax - k + tm, cols]
        count = jnp.minimum(pos + 1, w).astype(F32)
        pooled = acc / count - u
        mixed = jnp.dot(pooled.astype(BF16), wg_ref[g], preferred_element_type=F32) * sc_ref[:, cols]
        o_ref[:, cols] = mixed.astype(BF16)
    ext_ref[0:wmax, :] = ext_ref[tm:tm + wmax, :]


def _pool_mixer(hb, seq, w_in, w_group, scale):
    t, d = hb.shape
    tm = min(ROW_TILE, seq)
    ng, gd = w_group.shape[0], w_group.shape[1]
    return pl.pallas_call(
        functools.partial(_pool_body, seq // tm),
        grid=(t // tm,),
        in_specs=[
            pl.BlockSpec((tm, d), lambda i: (i, 0)),
            pl.BlockSpec((d, d), lambda i: (0, 0)),
            pl.BlockSpec((ng, gd, gd), lambda i: (0, 0, 0)),
            pl.BlockSpec((1, d), lambda i: (0, 0)),
        ],
        out_specs=pl.BlockSpec((tm, d), lambda i: (i, 0)),
        out_shape=jax.ShapeDtypeStruct((t, d), BF16),
        scratch_shapes=[pltpu.VMEM((tm + POOL_MAX_WINDOW, d), F32)],
        compiler_params=_params(("arbitrary",)),
        name="pool_mixer",
    )(hb, w_in, w_group, scale.reshape(1, d))


def _cmul(ar, ai, br, bi):
    return ar * br - ai * bi, ar * bi + ai * br


def _ssm_operators(lam_re, lam_im, log_step, b_re, b_im, c_re, c_im, d_skip, lc, n_blocks):
    f32 = F32
    lr, li = lam_re.astype(f32), lam_im.astype(f32)
    dt = jnp.exp(log_step.astype(f32))[:, None]
    mag = jnp.exp(dt * lr)
    abar_r, abar_i = mag * jnp.cos(dt * li), mag * jnp.sin(dt * li)
    den = lr * lr + li * li
    k_r = ((abar_r - 1.0) * lr + abar_i * li) / den
    k_i = (abar_i * lr - (abar_r - 1.0) * li) / den
    bb_r, bb_i = _cmul(k_r[..., None], k_i[..., None], b_re.astype(f32), b_im.astype(f32))
    cr, ci = c_re.astype(f32), c_im.astype(f32)
    g_n, p_n = lr.shape
    c_n = bb_r.shape[-1]

    def pw_step(carry, _):
        pr, pi = carry
        nr, ni = _cmul(pr, pi, abar_r, abar_i)
        return (nr, ni), (pr, pi)

    (_, _), (pw_r, pw_i) = lax.scan(pw_step, (jnp.ones_like(abar_r), jnp.zeros_like(abar_i)), None, length=lc + 1)
    ca_r = cr[None] * pw_r[:, :, None, :] - ci[None] * pw_i[:, :, None, :]
    ca_i = cr[None] * pw_i[:, :, None, :] + ci[None] * pw_r[:, :, None, :]
    kern = jnp.einsum('tgop,gpc->tgoc', ca_r[:lc], bb_r) - jnp.einsum('tgop,gpc->tgoc', ca_i[:lc], bb_i)
    eye = jnp.eye(c_n, dtype=f32)
    kern = kern.at[0].add(d_skip.astype(f32).reshape(g_n, c_n)[:, :, None] * eye[None])
    s_idx = jnp.arange(lc)[:, None]
    t_idx = jnp.arange(lc)[None, :]
    lag = t_idx - s_idx
    toep = kern[jnp.clip(lag, 0, lc - 1)] * (lag >= 0)[:, :, None, None, None].astype(f32)
    intra = jnp.transpose(toep, (2, 0, 4, 1, 3)).reshape(g_n, lc * c_n, lc * c_n)
    rev_r, rev_i = pw_r[lc - 1 - jnp.arange(lc)], pw_i[lc - 1 - jnp.arange(lc)]
    ts_r, ts_i = _cmul(rev_r[..., None], rev_i[..., None], bb_r[None], bb_i[None])
    to_state = jnp.concatenate([jnp.transpose(ts_r, (1, 0, 3, 2)), jnp.transpose(ts_i, (1, 0, 3, 2))], axis=-1)
    to_state = to_state.reshape(g_n, lc * c_n, 2 * p_n)
    fs_r = jnp.transpose(ca_r[1:lc + 1], (1, 3, 0, 2)).reshape(g_n, p_n, lc * c_n)
    fs_i = jnp.transpose(ca_i[1:lc + 1], (1, 3, 0, 2)).reshape(g_n, p_n, lc * c_n)
    from_state = jnp.concatenate([fs_r, -fs_i], axis=1)
    n_steps = max(1, (n_blocks - 1).bit_length())
    rows = []
    ar, ai = pw_r[lc], pw_i[lc]
    for _ in range(n_steps):
        rows.append(jnp.stack([jnp.concatenate([ar, ar], -1), jnp.concatenate([-ai, ai], -1)], axis=1))
        ar, ai = _cmul(ar, ai, ar, ai)
    apow = jnp.stack(rows, axis=1)
    return intra.astype(BF16), to_state.astype(BF16), from_state.astype(BF16), apow


def _ssm_body(n_blocks, u_ref, intra_ref, ts_ref, fs_ref, apow_ref, o_ref):
    u = u_ref[0]
    y = jnp.dot(u, intra_ref[0], preferred_element_type=F32)
    hs = jnp.dot(u, ts_ref[0], preferred_element_type=F32)
    rows = hs.shape[0]
    half = hs.shape[1] // 2
    blk = lax.broadcasted_iota(jnp.int32, (rows, 1), 0) % n_blocks
    d = 1
    k = 0
    while d < n_blocks:
        sh = jnp.where(blk >= d, pltpu.roll(hs, d, axis=0), 0.0)
        hs = hs + apow_ref[0, k, 0:1, :] * sh + apow_ref[0, k, 1:2, :] * pltpu.roll(sh, half, axis=1)
        d *= 2
        k += 1
    h_in = jnp.where(blk >= 1, pltpu.roll(hs, 1, axis=0), 0.0)
    y = y + jnp.dot(h_in.astype(BF16), fs_ref[0], preferred_element_type=F32)
    o_ref[0] = _gelu(y).astype(BF16)


def _ssm_scan(u_g, intra, to_state, from_state, apow, n_blocks):
    g_n, rows, width = u_g.shape
    p2 = to_state.shape[2]
    n_steps = apow.shape[1]
    return pl.pallas_call(
        functools.partial(_ssm_body, n_blocks),
        grid=(g_n,),
        in_specs=[
            pl.BlockSpec((1, rows, width), lambda g: (g, 0, 0)),
            pl.BlockSpec((1, width, width), lambda g: (g, 0, 0)),
            pl.BlockSpec((1, width, p2), lambda g: (g, 0, 0)),
            pl.BlockSpec((1, p2, width), lambda g: (g, 0, 0)),
            pl.BlockSpec((1, n_steps, 2, p2), lambda g: (g, 0, 0, 0)),
        ],
        out_specs=pl.BlockSpec((1, rows, width), lambda g: (g, 0, 0)),
        out_shape=jax.ShapeDtypeStruct((g_n, rows, width), BF16),
        compiler_params=_params(("arbitrary",)),
        name="ssm_scan",
    )(u_g, intra, to_state, from_state, apow)


def _glu_body(g_ref, w_ref, o_ref):
    g = g_ref[...]
    z = jnp.dot(g, w_ref[...], preferred_element_type=F32)
    o_ref[...] = (g.astype(F32) * jax.nn.sigmoid(z)).astype(BF16)


def _glu(g, w):
    t, d = g.shape
    tm = min(ROW_TILE, t)
    return pl.pallas_call(
        _glu_body,
        grid=(t // tm,),
        in_specs=[pl.BlockSpec((tm, d), lambda i: (i, 0)), pl.BlockSpec((d, d), lambda i: (0, 0))],
        out_specs=pl.BlockSpec((tm, d), lambda i: (i, 0)),
        out_shape=jax.ShapeDtypeStruct((t, d), BF16),
        compiler_params=_params(("arbitrary",)),
        name="ssm_glu",
    )(g, w)


def _ssm_mixer(hb, bsz, seq, w_in, ops, w_glu):
    t, d = hb.shape
    lc = SSM_LC
    n_blocks = seq // lc
    g_n = d // SSM_GROUP_DIM
    u = _matmul(hb, w_in, BF16)
    u_g = u.reshape(bsz * n_blocks, lc, g_n, SSM_GROUP_DIM).transpose(2, 0, 1, 3).reshape(g_n, bsz * n_blocks, lc * SSM_GROUP_DIM)
    y_g = _ssm_scan(u_g, *ops, n_blocks)
    gl = y_g.reshape(g_n, bsz * n_blocks, lc, SSM_GROUP_DIM).transpose(1, 2, 0, 3).reshape(t, d)
    return _glu(gl, w_glu)


def _attn_tables(rel_bias, tq):
    rb = rel_bias.astype(F32)
    heads = rb.shape[0]
    period = 2 * tq - 1

    def toeplitz(offset):
        m = np.arange(period)
        diff = np.where(m < tq, -m, period - m) + offset
        diag = rb[:, np.clip(diff, -ATTN_MAX_REL, ATTN_MAX_REL) + ATTN_MAX_REL]
        rows = jnp.tile(diag, (1, tq))[:, :tq * (period - 1)].reshape(heads, tq, period - 1)
        return rows[:, :, :tq]

    qc = np.arange(tq)[:, None] // CHUNK
    kc = np.arange(tq)[None, :] // CHUNK
    cur = jnp.where((kc <= qc)[None], toeplitz(0), NEG_INF)
    prev = jnp.where((kc >= qc)[None], toeplitz(tq), NEG_INF)
    return prev, cur


def _attn_body(scale, q_ref, kp_ref, kc_ref, vp_ref, vc_ref, bp_ref, bc_ref, o_ref, s_scr, p_scr, l_scr):
    i = pl.program_id(2)
    tq = q_ref.shape[0]
    rows_n = ATTN_ROWS
    width = tq + rows_n
    contract_last = (((1,), (1,)), ((), ()))
    for r in range(tq // rows_n):
        rows = slice(r * rows_n, (r + 1) * rows_n)
        lo, hi, pw = r * rows_n, (r + 1) * rows_n, tq - r * rows_n
        q = q_ref[rows, :]
        s_p = lax.dot_general(q, kp_ref[lo:, :], contract_last, preferred_element_type=F32) * scale + bp_ref[0, rows, lo:]
        s_scr[rows, 0:pw] = jnp.where(i == 0, NEG_INF, s_p)
        s_scr[rows, pw:width] = (lax.dot_general(q, kc_ref[:hi, :], contract_last, preferred_element_type=F32) * scale
                                 + bc_ref[0, rows, :hi])
    sub = 64
    for r in range(tq // sub):
        rows = slice(r * sub, (r + 1) * sub)
        s = s_scr[rows, :]
        p = jnp.exp(s - jnp.max(s, axis=-1, keepdims=True))
        l_scr[rows, :] = jnp.broadcast_to(jnp.sum(p, axis=-1, keepdims=True), (sub, LANES))
        p_scr[rows, :] = p.astype(BF16)
    for r in range(tq // rows_n):
        rows = slice(r * rows_n, (r + 1) * rows_n)
        lo, hi, pw = r * rows_n, (r + 1) * rows_n, tq - r * rows_n
        o = jnp.dot(p_scr[rows, 0:pw], vp_ref[lo:, :], preferred_element_type=F32)
        o = o + jnp.dot(p_scr[rows, pw:width], vc_ref[:hi, :], preferred_element_type=F32)
        o_ref[rows, :] = (o / l_scr[rows, :]).astype(BF16)


def _chunk_attention(qkv, bsz, seq, heads, bias_prev, bias_cur):
    t = qkv.shape[0]
    hd = qkv.shape[1] // (3 * heads)
    tq = ATTN_TQ
    nt = seq // tq
    cur = lambda h, b, i: (b * nt + i, 0)
    prev = lambda h, b, i: (b * nt + jnp.maximum(i - 1, 0), 0)

    def col(base, rows):
        return lambda h, b, i: (rows(h, b, i)[0], base + h)

    return pl.pallas_call(
        functools.partial(_attn_body, hd ** -0.5),
        grid=(heads, bsz, nt),
        in_specs=[
            pl.BlockSpec((tq, hd), col(0, cur)),
            pl.BlockSpec((tq, hd), col(heads, prev)),
            pl.BlockSpec((tq, hd), col(heads, cur)),
            pl.BlockSpec((tq, hd), col(2 * heads, prev)),
            pl.BlockSpec((tq, hd), col(2 * heads, cur)),
            pl.BlockSpec((1, tq, tq), lambda h, b, i: (h, 0, 0)),
            pl.BlockSpec((1, tq, tq), lambda h, b, i: (h, 0, 0)),
        ],
        out_specs=pl.BlockSpec((tq, hd), lambda h, b, i: (b * nt + i, h)),
        out_shape=jax.ShapeDtypeStruct((t, heads * hd), BF16),
        scratch_shapes=[
            pltpu.VMEM((tq, tq + ATTN_ROWS), F32),
            pltpu.VMEM((tq, tq + ATTN_ROWS), BF16),
            pltpu.VMEM((tq, LANES), F32),
        ],
        compiler_params=_params(("arbitrary", "arbitrary", "arbitrary")),
        name="chunk_attention",
    )(qkv, qkv, qkv, qkv, qkv, bias_prev, bias_cur)


def _sort_pairs(n):
    pairs = []
    p = 1
    while p < n:
        k = p
        while k >= 1:
            for j in range(k % p, n - k, 2 * k):
                for i in range(min(k, n - j - k)):
                    if (i + j) // (2 * p) == (i + j + k) // (2 * p):
                        pairs.append((i + j, i + j + k))
            k //= 2
        p *= 2
    return pairs


def _cmp_exchange(x, y):
    if x is None:
        return y, None
    if y is None:
        return x, None
    return jnp.maximum(x, y), jnp.minimum(x, y)


def _opt_max(x, y):
    if x is None:
        return y
    if y is None:
        return x
    return jnp.maximum(x, y)


def _merge_top(a, a_next, b, b_next):
    k = len(a)
    hi, lo_max = [], None
    for i in range(k):
        h, l = _cmp_exchange(a[i], b[k - 1 - i])
        hi.append(h)
        lo_max = _opt_max(lo_max, l)
    d = k // 2
    while d >= 1:
        for i in range(k):
            if i & d == 0:
                hi[i], hi[i + d] = _cmp_exchange(hi[i], hi[i + d])
        d //= 2
    return hi, _opt_max(_opt_max(lo_max, a_next), b_next)


def _top_sorted(vals, k):
    vals = list(vals) + [None] * (-len(vals) % k)
    pairs = _sort_pairs(k)
    groups = []
    for g in range(0, len(vals), k):
        v = vals[g:g + k]
        for i, j in pairs:
            v[i], v[j] = _cmp_exchange(v[i], v[j])
        groups.append((v, None))
    while len(groups) > 1:
        merged = [_merge_top(*groups[i], *groups[i + 1]) for i in range(0, len(groups) - 1, 2)]
        groups = merged + groups[len(groups) - len(groups) % 2:]
    return groups[0]


def _peer_score_body(q_ref, keys_ref, e2_ref, f1_ref, thr_ref, km1_scr, km2_scr, stat_scr):
    k = PEER_TOPK
    nk = keys_ref.shape[1]
    half = keys_ref.shape[2]
    ts = q_ref.shape[1]
    groups = ts // LANES
    assert groups == 8, "one sublane per 128-token group"

    def head(h, carry):
        base = pl.multiple_of(h * 2 * half, 2 * half)
        s1 = jnp.dot(keys_ref[0], q_ref[pl.ds(base, half), :], preferred_element_type=F32)
        s2 = jnp.dot(keys_ref[1], q_ref[pl.ds(base + half, half), :], preferred_element_type=F32)
        for g in range(groups):
            km1_scr[g * nk:(g + 1) * nk, :] = s1[:, g * LANES:(g + 1) * LANES]
            km2_scr[g * nk:(g + 1) * nk, :] = s2[:, g * LANES:(g + 1) * LANES]
        v1 = [km1_scr[pl.ds(key, groups, stride=nk), :] for key in range(nk)]
        v2 = [km2_scr[pl.ds(key, groups, stride=nk), :] for key in range(nk)]
        a, a_next = _top_sorted(v1, k)
        b, b_next = _top_sorted(v2, k)
        cand = [a[i] + b[j] for i in range(k) for j in range(k) if (i + 1) * (j + 1) <= k]
        cand += [a_next + b[0], a[0] + b_next]
        best, nxt = _top_sorted(cand, k)
        z = 1.0 + jnp.exp(best[1] - best[0])
        for r in range(2, k):
            z = z + jnp.exp(best[r] - best[0])
        stat_scr[0 * groups:1 * groups, :] = a[0]
        stat_scr[1 * groups:2 * groups, :] = b[0]
        stat_scr[2 * groups:3 * groups, :] = z
        stat_scr[3 * groups:4 * groups, :] = 0.5 * (best[k - 1] + nxt)
        for g in range(groups):
            cols = slice(g * LANES, (g + 1) * LANES)
            m1 = stat_scr[0 * groups + g:0 * groups + g + 1, :]
            m2 = stat_scr[1 * groups + g:1 * groups + g + 1, :]
            zg = stat_scr[2 * groups + g:2 * groups + g + 1, :]
            tau = stat_scr[3 * groups + g:3 * groups + g + 1, :]
            s1g = km1_scr[g * nk:(g + 1) * nk, :]
            s2g = km2_scr[g * nk:(g + 1) * nk, :]
            e2_ref[h, :, cols] = jnp.exp(s2g - m2).astype(BF16)
            f1_ref[h, :, cols] = jnp.exp(s1g - m1) / zg
            thr_ref[h, :, cols] = jnp.exp(tau - s1g - m2)
        return carry

    lax.fori_loop(0, PEER_HEADS, head, 0)


def _peer_scores(qt, keys):
    qd, t = qt.shape
    ts = PEER_TS
    nk = keys.shape[1]
    groups = ts // LANES
    spec = pl.BlockSpec((PEER_HEADS, nk, ts), lambda i: (0, 0, i))
    shape = jax.ShapeDtypeStruct((PEER_HEADS, nk, t), F32)
    return pl.pallas_call(
        _peer_score_body,
        grid=(t // ts,),
        in_specs=[
            pl.BlockSpec((qd, ts), lambda i: (0, i)),
            pl.BlockSpec((2, nk, PEER_HALF), lambda i: (0, 0, 0)),
        ],
        out_specs=[spec, spec, spec],
        out_shape=[jax.ShapeDtypeStruct(shape.shape, BF16), shape, shape],
        scratch_shapes=[
            pltpu.VMEM((groups * nk, LANES), F32),
            pltpu.VMEM((groups * nk, LANES), F32),
            pltpu.VMEM((4 * groups, LANES), F32),
        ],
        compiler_params=_params(("arbitrary",)),
        name="peer_scores",
    )(qt, keys)


def _peer_stage_rows(thr_ref, f1_ref, rows_ref, base, blk, n_rows):
    last_row = thr_ref.shape[1] - 1
    for jj in range(n_rows):
        row = jnp.clip(blk * n_rows + jj, 0, last_row)
        for h in range(PEER_HEADS):
            at = base + (jj * 2) * PEER_HEADS + h
            rows_ref[at:at + 1, :] = thr_ref[h, pl.ds(row, 1), :]
            rows_ref[at + PEER_HEADS:at + PEER_HEADS + 1, :] = f1_ref[h, pl.ds(row, 1), :]


def _peer_gate_piece(e2_ref, rows_ref, base, st_ref, wp_ref, jj, ls):
    nk = e2_ref.shape[1]
    rc_n = PEER_RC
    width = ls.stop - ls.start
    at = base + (jj * 2) * PEER_HEADS
    thr = [jnp.broadcast_to(rows_ref[at + h:at + h + 1, ls], (rc_n, width)).astype(BF16) for h in range(PEER_HEADS)]
    f1 = [jnp.broadcast_to(rows_ref[at + PEER_HEADS + h:at + PEER_HEADS + h + 1, ls], (rc_n, width)).astype(BF16)
          for h in range(PEER_HEADS)]
    zero = jnp.zeros((rc_n, width), BF16)
    for rc in range(nk // rc_n):
        rs = slice(rc * rc_n, (rc + 1) * rc_n)
        wg = None
        for h in range(PEER_HEADS):
            e = e2_ref[h, rs, ls]
            part = jnp.where(e >= thr[h], e, zero) * f1[h]
            wg = part if wg is None else wg + part
        es = slice(jj * nk + rc * rc_n, jj * nk + (rc + 1) * rc_n)
        wp_ref[es, ls] = wg * _gelu(st_ref[es, ls]).astype(BF16)


def _peer_half_step(xb_ref, u_ref, vt_ref, cols, e2_ref, rows_ref, base,
                    st_out_ref, st_in_ref, wp_out_ref, wp_in_ref, acc_ref):
    nk = e2_ref.shape[1]
    n_rows = st_in_ref.shape[0] // nk
    width = PEER_GATE_LANES
    n_pc = st_in_ref.shape[1] // width
    rows_per = acc_ref.shape[0] // n_pc
    contract_last = (((1,), (1,)), ((), ()))
    for pc in range(n_pc):
        ls = slice(pc * width, (pc + 1) * width)
        st_out_ref[:, ls] = lax.dot_general(u_ref[0, cols, :], xb_ref[ls, :], contract_last,
                                            preferred_element_type=F32)
        for jj in range(n_rows):
            _peer_gate_piece(e2_ref, rows_ref, base, st_in_ref, wp_out_ref, jj, ls)
        dr = slice(pc * rows_per, (pc + 1) * rows_per)
        acc_ref[dr, :] += jnp.dot(vt_ref[0, dr, cols], wp_in_ref[...], preferred_element_type=F32)


def _peer_dense_body(alpha, xb_ref, u_ref, vt_ref, e2_ref, f1_ref, thr_ref, res_ref, g_ref, b_ref,
                     h_ref, hb_ref, acc_ref, st0_ref, st1_ref, wp0_ref, wp1_ref, rows_ref):
    j = pl.program_id(1)
    nk = e2_ref.shape[1]
    half = u_ref.shape[1] // 2
    n_rows = half // nk

    @pl.when(j == 0)
    def _():
        acc_ref[...] = jnp.zeros(acc_ref.shape, F32)
        st1_ref[...] = jnp.zeros(st1_ref.shape, F32)
        wp0_ref[...] = jnp.zeros(wp0_ref.shape, BF16)

    per_half = n_rows * 2 * PEER_HEADS
    _peer_stage_rows(thr_ref, f1_ref, rows_ref, 0, 2 * j - 1, n_rows)
    _peer_stage_rows(thr_ref, f1_ref, rows_ref, per_half, 2 * j, n_rows)

    _peer_half_step(xb_ref, u_ref, vt_ref, slice(0, half), e2_ref, rows_ref, 0,
                    st0_ref, st1_ref, wp1_ref, wp0_ref, acc_ref)
    _peer_half_step(xb_ref, u_ref, vt_ref, slice(half, 2 * half), e2_ref, rows_ref, per_half,
                    st1_ref, st0_ref, wp0_ref, wp1_ref, acc_ref)

    @pl.when(j == pl.num_programs(1) - 1)
    def _():
        y = _layer_norm(alpha * res_ref[...] + acc_ref[...].T, g_ref[...], b_ref[...])
        h_ref[...] = y
        hb_ref[...] = y.astype(BF16)


def _peer_dense(xb, u, vt, e2, f1, thr, res, g, b, alpha):
    t, d = xb.shape
    nj, eb, _ = u.shape
    nk = e2.shape[1]
    tt = min(PEER_TT, t)
    hds = PEER_HEADS
    once = pl.Buffered(1)
    return pl.pallas_call(
        functools.partial(_peer_dense_body, alpha),
        grid=(t // tt, nj + 1),
        in_specs=[
            pl.BlockSpec((tt, d), lambda i, j: (i, 0)),
            pl.BlockSpec((1, eb, d), lambda i, j: (jnp.minimum(j, nj - 1), 0, 0)),
            pl.BlockSpec((1, d, eb), lambda i, j: (jnp.maximum(j - 1, 0), 0, 0)),
            pl.BlockSpec((hds, nk, tt), lambda i, j: (0, 0, i)),
            pl.BlockSpec((hds, nk, tt), lambda i, j: (0, 0, i), pipeline_mode=once),
            pl.BlockSpec((hds, nk, tt), lambda i, j: (0, 0, i), pipeline_mode=once),
            pl.BlockSpec((tt, d), lambda i, j: (i, 0), pipeline_mode=once),
            pl.BlockSpec((1, d), lambda i, j: (0, 0)),
            pl.BlockSpec((1, d), lambda i, j: (0, 0)),
        ],
        out_specs=[pl.BlockSpec((tt, d), lambda i, j: (i, 0)), pl.BlockSpec((tt, d), lambda i, j: (i, 0))],
        out_shape=[jax.ShapeDtypeStruct((t, d), F32), jax.ShapeDtypeStruct((t, d), BF16)],
        scratch_shapes=[
            pltpu.VMEM((d, tt), F32),
            pltpu.VMEM((eb // 2, tt), F32),
            pltpu.VMEM((eb // 2, tt), F32),
            pltpu.VMEM((eb // 2, tt), BF16),
            pltpu.VMEM((eb // 2, tt), BF16),
            pltpu.VMEM((2 * (eb // 2 // nk) * 2 * hds, tt), F32),
        ],
        compiler_params=_params(("arbitrary", "arbitrary")),
        name="peer_dense",
    )(xb, u, vt, e2, f1, thr, res, g.reshape(1, d), b.reshape(1, d))


def _expert_blocks(table, transpose):
    ne, d = table.shape
    blocks = table.astype(BF16).reshape(ne // PEER_EB, PEER_EB, d)
    return blocks.transpose(0, 2, 1) if transpose else blocks


def kernel(x, p, ln_mix_g, ln_mix_b, ln_ffn_g, ln_ffn_b, pool_w_in, pool_w_group, pool_scale, pool_w_out, ssm_w_in, ssm_lambda_re, ssm_lambda_im, ssm_log_step, ssm_b_re, ssm_b_im, ssm_c_re, ssm_c_im, ssm_d, ssm_w_glu, ssm_w_out, attn_w_qkv, attn_rel_bias, attn_w_out, peer_w_q, peer_sub_keys, peer_u, peer_v, ple_w_proj, ple_w_gate, ple_b_gate):
    bsz, seq, d = x.shape
    depth = p.shape[0]
    t = bsz * seq
    alpha = (2.0 * depth) ** 0.25
    h = x.reshape(t, d)
    hb = h.astype(BF16)
    for i in range(depth):
        kind, j = i % N_MIXERS, i // N_MIXERS
        if kind == 0:
            mixed = _pool_mixer(hb, seq, pool_w_in[j].astype(BF16), pool_w_group[j].astype(BF16), pool_scale[j])
            w_out = pool_w_out[j]
        elif kind == 1:
            ops = _ssm_operators(ssm_lambda_re[j], ssm_lambda_im[j], ssm_log_step[j], ssm_b_re[j], ssm_b_im[j],
                                 ssm_c_re[j], ssm_c_im[j], ssm_d[j], SSM_LC, seq // SSM_LC)
            mixed = _ssm_mixer(hb, bsz, seq, ssm_w_in[j].astype(BF16), ops, ssm_w_glu[j].astype(BF16))
            w_out = ssm_w_out[j]
        else:
            qkv = _matmul(hb, attn_w_qkv[j].astype(BF16), BF16)
            bias_prev, bias_cur = _attn_tables(attn_rel_bias[j], ATTN_TQ)
            mixed = _chunk_attention(qkv, bsz, seq, ATTN_HEADS, bias_prev, bias_cur)
            w_out = attn_w_out[j]
        h, hb, ht = _matmul_ln(mixed, w_out.astype(BF16), h, ln_mix_g[i], ln_mix_b[i], alpha)
        qt = _matmul(peer_w_q[i].T.astype(BF16), ht, BF16)
        e2, f1, thr = _peer_scores(qt, peer_sub_keys[i].astype(BF16))
        h, hb = _peer_dense(hb, _expert_blocks(peer_u[i], False), _expert_blocks(peer_v[i], True), e2, f1, thr,
                            h, ln_ffn_g[i], ln_ffn_b[i], alpha)
        h, hb = _ple(hb, h, p[i].reshape(t, -1).astype(BF16), ple_w_gate[i].astype(BF16),
                     ple_w_proj[i].astype(BF16), ple_b_gate[i])
    return h.reshape(bsz, seq, d)
```

```python
import functools
import math

import jax
import jax.numpy as jnp
import numpy as np
from jax import lax
from jax.experimental import pallas as pl
from jax.experimental.pallas import tpu as pltpu

F32 = jnp.float32
BF16 = jnp.bfloat16

CHUNK = 64
N_MIXERS = 3
LN_EPS = 1e-5
POOL_WINDOWS = (2, 4, 8, 16)
POOL_MAX_WINDOW = max(POOL_WINDOWS)
SSM_GROUP_DIM = 16
SSM_STATE = 64
ATTN_HEADS = 16
ATTN_LEFT_CHUNKS = 8
ATTN_MAX_REL = 2 * CHUNK
NEG_INF = -1e30
PEER_N_KEYS = 128
PEER_HEADS = 8
PEER_HALF = 128
PEER_TOPK = 16

LANES = 128
SUBLANES = 8
VMEM_LIMIT = 56 * 1024 * 1024
ROW_TILE = 512
MM_TM, MM_TN = 1024, 1024
ATTN_TQ = ATTN_LEFT_CHUNKS * CHUNK
ATTN_ROWS = LANES
PEER_TS = SUBLANES * LANES
PEER_TT = 512
PEER_EB = 512
PEER_RC = 8
PEER_GATE_LANES = 2 * LANES
SSM_LC = 32


def _params(sem):
    return pltpu.CompilerParams(dimension_semantics=sem, vmem_limit_bytes=VMEM_LIMIT)


def _divisor_tile(n, preferred):
    best = LANES
    for cand in range(LANES, min(n, preferred) + 1, LANES):
        if n % cand == 0:
            best = cand
    assert n % best == 0, (n, preferred)
    return best


def _gelu(x):
    return 0.5 * x * (1.0 + lax.erf(x * (1.0 / math.sqrt(2.0))))


def _layer_norm(z, g, b):
    mu = jnp.mean(z, axis=-1, keepdims=True)
    zc = z - mu
    var = jnp.mean(zc * zc, axis=-1, keepdims=True)
    return zc * lax.rsqrt(var + LN_EPS) * g + b


def _mm_body(a_ref, w_ref, o_ref):
    o_ref[...] = jnp.dot(a_ref[...], w_ref[...], preferred_element_type=F32).astype(o_ref.dtype)


def _matmul(a, w, out_dtype):
    m, k = a.shape
    n = w.shape[1]
    tm, tn = _divisor_tile(m, MM_TM), _divisor_tile(n, MM_TN)
    return pl.pallas_call(
        _mm_body,
        grid=(n // tn, m // tm),
        in_specs=[pl.BlockSpec((tm, k), lambda j, i: (i, 0)), pl.BlockSpec((k, tn), lambda j, i: (0, j))],
        out_specs=pl.BlockSpec((tm, tn), lambda j, i: (i, j)),
        out_shape=jax.ShapeDtypeStruct((m, n), out_dtype),
        compiler_params=_params(("arbitrary", "arbitrary")),
        name="matmul",
    )(a, w)


def _mm_ln_body(alpha, a_ref, w_ref, res_ref, g_ref, b_ref, h_ref, hb_ref, ht_ref):
    m = jnp.dot(a_ref[...], w_ref[...], preferred_element_type=F32)
    y = _layer_norm(alpha * res_ref[...] + m, g_ref[...], b_ref[...])
    h_ref[...] = y
    hb_ref[...] = y.astype(BF16)
    ht_ref[...] = y.T.astype(BF16)


def _matmul_ln(a, w, res, g, b, alpha):
    t, k = a.shape
    d = w.shape[1]
    tm = min(ROW_TILE, t)
    return pl.pallas_call(
        functools.partial(_mm_ln_body, alpha),
        grid=(t // tm,),
        in_specs=[
            pl.BlockSpec((tm, k), lambda i: (i, 0)),
            pl.BlockSpec((k, d), lambda i: (0, 0)),
            pl.BlockSpec((tm, d), lambda i: (i, 0)),
            pl.BlockSpec((1, d), lambda i: (0, 0)),
            pl.BlockSpec((1, d), lambda i: (0, 0)),
        ],
        out_specs=[pl.BlockSpec((tm, d), lambda i: (i, 0)), pl.BlockSpec((tm, d), lambda i: (i, 0)),
                   pl.BlockSpec((d, tm), lambda i: (0, i))],
        out_shape=[jax.ShapeDtypeStruct((t, d), F32), jax.ShapeDtypeStruct((t, d), BF16),
                   jax.ShapeDtypeStruct((d, t), BF16)],
        compiler_params=_params(("arbitrary",)),
        name="outproj_ln",
    )(a, w, res, g.reshape(1, d), b.reshape(1, d))


def _ple_body(hb_ref, h_ref, p_ref, wg_ref, wp_ref, bg_ref, o_ref, ob_ref):
    gate = jax.nn.sigmoid(jnp.dot(hb_ref[...], wg_ref[...], preferred_element_type=F32) + bg_ref[...])
    proj = jnp.dot(p_ref[...], wp_ref[...], preferred_element_type=F32)
    y = h_ref[...] + gate * proj
    o_ref[...] = y
    ob_ref[...] = y.astype(BF16)


def _ple(hb, h, p, wg, wp, bg):
    t, d = h.shape
    pd = p.shape[1]
    tm = min(ROW_TILE, t)
    return pl.pallas_call(
        _ple_body,
        grid=(t // tm,),
        in_specs=[
            pl.BlockSpec((tm, d), lambda i: (i, 0)),
            pl.BlockSpec((tm, d), lambda i: (i, 0)),
            pl.BlockSpec((tm, pd), lambda i: (i, 0)),
            pl.BlockSpec((d, d), lambda i: (0, 0)),
            pl.BlockSpec((pd, d), lambda i: (0, 0)),
            pl.BlockSpec((1, d), lambda i: (0, 0)),
        ],
        out_specs=[pl.BlockSpec((tm, d), lambda i: (i, 0)), pl.BlockSpec((tm, d), lambda i: (i, 0))],
        out_shape=[jax.ShapeDtypeStruct((t, d), F32), jax.ShapeDtypeStruct((t, d), BF16)],
        compiler_params=_params(("arbitrary",)),
        name="ple_gate",
    )(hb, h, p, wg, wp, bg.reshape(1, d))


def _pool_body(tiles_per_seq, x_ref, win_ref, wg_ref, sc_ref, o_ref, ext_ref):
    wmax = POOL_MAX_WINDOW
    tm = x_ref.shape[0]
    gd = wg_ref.shape[1]
    i = pl.program_id(0)

    @pl.when(i % tiles_per_seq == 0)
    def _():
        ext_ref[0:wmax, :] = jnp.zeros((wmax, ext_ref.shape[1]), F32)

    ext_ref[wmax:wmax + tm, :] = jnp.dot(x_ref[...], win_ref[...], preferred_element_type=F32)
    pos = (i % tiles_per_seq) * tm + lax.broadcasted_iota(jnp.int32, (tm, 1), 0)
    for g, w in enumerate(POOL_WINDOWS):
        cols = slice(g * gd, (g + 1) * gd)
        u = ext_ref[wmax:wmax + tm, cols]
        acc = u
        for k in range(1, w):
            acc = acc + ext_ref[wmax - k:wmax - k + tm, cols]
        count = jnp.minimum(pos + 1, w).astype(F32)
        pooled = acc / count - u
        mixed = jnp.dot(pooled.astype(BF16), wg_ref[g], preferred_element_type=F32) * sc_ref[:, cols]
        o_ref[:, cols] = mixed.astype(BF16)
    ext_ref[0:wmax, :] = ext_ref[tm:tm + wmax, :]


def _pool_mixer(hb, seq, w_in, w_group, scale):
    t, d = hb.shape
    tm = min(ROW_TILE, seq)
    ng, gd = w_group.shape[0], w_group.shape[1]
    return pl.pallas_call(
        functools.partial(_pool_body, seq // tm),
        grid=(t // tm,),
        in_specs=[
            pl.BlockSpec((tm, d), lambda i: (i, 0)),
            pl.BlockSpec((d, d), lambda i: (0, 0)),
            pl.BlockSpec((ng, gd, gd), lambda i: (0, 0, 0)),
            pl.BlockSpec((1, d), lambda i: (0, 0)),
        ],
        out_specs=pl.BlockSpec((tm, d), lambda i: (i, 0)),
        out_shape=jax.ShapeDtypeStruct((t, d), BF16),
        scratch_shapes=[pltpu.VMEM((tm + POOL_MAX_WINDOW, d), F32)],
        compiler_params=_params(("arbitrary",)),
        name="pool_mixer",
    )(hb, w_in, w_group, scale.reshape(1, d))


def _cmul(ar, ai, br, bi):
    return ar * br - ai * bi, ar * bi + ai * br


def _ssm_operators(lam_re, lam_im, log_step, b_re, b_im, c_re, c_im, d_skip, lc, n_blocks):
    f32 = F32
    lr, li = lam_re.astype(f32), lam_im.astype(f32)
    dt = jnp.exp(log_step.astype(f32))[:, None]
    mag = jnp.exp(dt * lr)
    abar_r, abar_i = mag * jnp.cos(dt * li), mag * jnp.sin(dt * li)
    den = lr * lr + li * li
    k_r = ((abar_r - 1.0) * lr + abar_i * li) / den
    k_i = (abar_i * lr - (abar_r - 1.0) * li) / den
    bb_r, bb_i = _cmul(k_r[..., None], k_i[..., None], b_re.astype(f32), b_im.astype(f32))
    cr, ci = c_re.astype(f32), c_im.astype(f32)
    g_n, p_n = lr.shape
    c_n = bb_r.shape[-1]

    def pw_step(carry, _):
        pr, pi = carry
        nr, ni = _cmul(pr, pi, abar_r, abar_i)
        return (nr, ni), (pr, pi)

    (_, _), (pw_r, pw_i) = lax.scan(pw_step, (jnp.ones_like(abar_r), jnp.zeros_like(abar_i)), None, length=lc + 1)
    ca_r = cr[None] * pw_r[:, :, None, :] - ci[None] * pw_i[:, :, None, :]
    ca_i = cr[None] * pw_i[:, :, None, :] + ci[None] * pw_r[:, :, None, :]
    kern = jnp.einsum('tgop,gpc->tgoc', ca_r[:lc], bb_r) - jnp.einsum('tgop,gpc->tgoc', ca_i[:lc], bb_i)
    eye = jnp.eye(c_n, dtype=f32)
    kern = kern.at[0].add(d_skip.astype(f32).reshape(g_n, c_n)[:, :, None] * eye[None])
    s_idx = jnp.arange(lc)[:, None]
    t_idx = jnp.arange(lc)[None, :]
    lag = t_idx - s_idx
    toep = kern[jnp.clip(lag, 0, lc - 1)] * (lag >= 0)[:, :, None, None, None].astype(f32)
    intra = jnp.transpose(toep, (2, 0, 4, 1, 3)).reshape(g_n, lc * c_n, lc * c_n)
    rev_r, rev_i = pw_r[lc - 1 - jnp.arange(lc)], pw_i[lc - 1 - jnp.arange(lc)]
    ts_r, ts_i = _cmul(rev_r[..., None], rev_i[..., None], bb_r[None], bb_i[None])
    to_state = jnp.concatenate([jnp.transpose(ts_r, (1, 0, 3, 2)), jnp.transpose(ts_i, (1, 0, 3, 2))], axis=-1)
    to_state = to_state.reshape(g_n, lc * c_n, 2 * p_n)
    fs_r = jnp.transpose(ca_r[1:lc + 1], (1, 3, 0, 2)).reshape(g_n, p_n, lc * c_n)
    fs_i = jnp.transpose(ca_i[1:lc + 1], (1, 3, 0, 2)).reshape(g_n, p_n, lc * c_n)
    from_state = jnp.concatenate([fs_r, -fs_i], axis=1)
    n_steps = max(1, (n_blocks - 1).bit_length())
    rows = []
    ar, ai = pw_r[lc], pw_i[lc]
    for _ in range(n_steps):
        rows.append(jnp.stack([jnp.concatenate([ar, ar], -1), jnp.concatenate([-ai, ai], -1)], axis=1))
        ar, ai = _cmul(ar, ai, ar, ai)
    apow = jnp.stack(rows, axis=1)
    return intra.astype(BF16), to_state.astype(BF16), from_state.astype(BF16), apow


def _ssm_body(n_blocks, u_ref, intra_ref, ts_ref, fs_ref, apow_ref, o_ref):
    u = u_ref[0]
    y = jnp.dot(u, intra_ref[0], preferred_element_type=F32)
    hs = jnp.dot(u, ts_ref[0], preferred_element_type=F32)
    rows = hs.shape[0]
    half = hs.shape[1] // 2
    blk = lax.broadcasted_iota(jnp.int32, (rows, 1), 0) % n_blocks
    d = 1
    k = 0
    while d < n_blocks:
        sh = jnp.where(blk >= d, pltpu.roll(hs, d, axis=0), 0.0)
        hs = hs + apow_ref[0, k, 0:1, :] * sh + apow_ref[0, k, 1:2, :] * pltpu.roll(sh, half, axis=1)
        d *= 2
        k += 1
    h_in = jnp.where(blk >= 1, pltpu.roll(hs, 1, axis=0), 0.0)
    y = y + jnp.dot(h_in.astype(BF16), fs_ref[0], preferred_element_type=F32)
    o_ref[0] = _gelu(y).astype(BF16)


def _ssm_scan(u_g, intra, to_state, from_state, apow, n_blocks):
    g_n, rows, width = u_g.shape
    p2 = to_state.shape[2]
    n_steps = apow.shape[1]
    return pl.pallas_call(
        functools.partial(_ssm_body, n_blocks),
        grid=(g_n,),
        in_specs=[
            pl.BlockSpec((1, rows, width), lambda g: (g, 0, 0)),
            pl.BlockSpec((1, width, width), lambda g: (g, 0, 0)),
            pl.BlockSpec((1, width, p2), lambda g: (g, 0, 0)),
            pl.BlockSpec((1, p2, width), lambda g: (g, 0, 0)),
            pl.BlockSpec((1, n_steps, 2, p2), lambda g: (g, 0, 0, 0)),
        ],
        out_specs=pl.BlockSpec((1, rows, width), lambda g: (g, 0, 0)),
        out_shape=jax.ShapeDtypeStruct((g_n, rows, width), BF16),
        compiler_params=_params(("arbitrary",)),
        name="ssm_scan",
    )(u_g, intra, to_state, from_state, apow)


def _glu_body(g_ref, w_ref, o_ref):
    g = g_ref[...]
    z = jnp.dot(g, w_ref[...], preferred_element_type=F32)
    o_ref[...] = (g.astype(F32) * jax.nn.sigmoid(z)).astype(BF16)


def _glu(g, w):
    t, d = g.shape
    tm = min(ROW_TILE, t)
    return pl.pallas_call(
        _glu_body,
        grid=(t // tm,),
        in_specs=[pl.BlockSpec((tm, d), lambda i: (i, 0)), pl.BlockSpec((d, d), lambda i: (0, 0))],
        out_specs=pl.BlockSpec((tm, d), lambda i: (i, 0)),
        out_shape=jax.ShapeDtypeStruct((t, d), BF16),
        compiler_params=_params(("arbitrary",)),
        name="ssm_glu",
    )(g, w)


def _ssm_mixer(hb, bsz, seq, w_in, ops, w_glu):
    t, d = hb.shape
    lc = SSM_LC
    n_blocks = seq // lc
    g_n = d // SSM_GROUP_DIM
    u = _matmul(hb, w_in, BF16)
    u_g = u.reshape(bsz * n_blocks, lc, g_n, SSM_GROUP_DIM).transpose(2, 0, 1, 3).reshape(g_n, bsz * n_blocks, lc * SSM_GROUP_DIM)
    y_g = _ssm_scan(u_g, *ops, n_blocks)
    gl = y_g.reshape(g_n, bsz * n_blocks, lc, SSM_GROUP_DIM).transpose(1, 2, 0, 3).reshape(t, d)
    return _glu(gl, w_glu)


def _attn_tables(rel_bias, tq):
    rb = rel_bias.astype(F32)
    heads = rb.shape[0]
    period = 2 * tq - 1

    def toeplitz(offset):
        m = np.arange(period)
        diff = np.where(m < tq, -m, period - m) + offset
        diag = rb[:, np.clip(diff, -ATTN_MAX_REL, ATTN_MAX_REL) + ATTN_MAX_REL]
        rows = jnp.tile(diag, (1, tq))[:, :tq * (period - 1)].reshape(heads, tq, period - 1)
        return rows[:, :, :tq]

    qc = np.arange(tq)[:, None] // CHUNK
    kc = np.arange(tq)[None, :] // CHUNK
    cur = jnp.where((kc <= qc)[None], toeplitz(0), NEG_INF)
    prev = jnp.where((kc >= qc)[None], toeplitz(tq), NEG_INF)
    return prev, cur


def _attn_body(scale, q_ref, kp_ref, kc_ref, vp_ref, vc_ref, bp_ref, bc_ref, o_ref, s_scr, p_scr, l_scr):
    i = pl.program_id(2)
    tq = q_ref.shape[0]
    rows_n = ATTN_ROWS
    width = tq + rows_n
    contract_last = (((1,), (1,)), ((), ()))
    for r in range(tq // rows_n):
        rows = slice(r * rows_n, (r + 1) * rows_n)
        lo, hi, pw = r * rows_n, (r + 1) * rows_n, tq - r * rows_n
        q = q_ref[rows, :]
        s_p = lax.dot_general(q, kp_ref[lo:, :], contract_last, preferred_element_type=F32) * scale + bp_ref[0, rows, lo:]
        s_scr[rows, 0:pw] = jnp.where(i == 0, NEG_INF, s_p)
        s_scr[rows, pw:width] = (lax.dot_general(q, kc_ref[:hi, :], contract_last, preferred_element_type=F32) * scale
                                 + bc_ref[0, rows, :hi])
    sub = CHUNK
    for r in range(tq // sub):
        rows = slice(r * sub, (r + 1) * sub)
        s = s_scr[rows, :]
        p = jnp.exp(s - jnp.max(s, axis=-1, keepdims=True))
        l_scr[rows, :] = jnp.broadcast_to(jnp.sum(p, axis=-1, keepdims=True), (sub, LANES))
        p_scr[rows, :] = p.astype(BF16)
    for r in range(tq // rows_n):
        rows = slice(r * rows_n, (r + 1) * rows_n)
        lo, hi, pw = r * rows_n, (r + 1) * rows_n, tq - r * rows_n
        o = jnp.dot(p_scr[rows, 0:pw], vp_ref[lo:, :], preferred_element_type=F32)
        o = o + jnp.dot(p_scr[rows, pw:width], vc_ref[:hi, :], preferred_element_type=F32)
        o_ref[rows, :] = (o / l_scr[rows, :]).astype(BF16)


def _chunk_attention(qkv, bsz, seq, heads, bias_prev, bias_cur):
    t = qkv.shape[0]
    hd = qkv.shape[1] // (3 * heads)
    tq = ATTN_TQ
    nt = seq // tq
    cur = lambda h, b, i: (b * nt + i, 0)
    prev = lambda h, b, i: (b * nt + jnp.maximum(i - 1, 0), 0)

    def col(base, rows):
        return lambda h, b, i: (rows(h, b, i)[0], base + h)

    return pl.pallas_call(
        functools.partial(_attn_body, hd ** -0.5),
        grid=(heads, bsz, nt),
        in_specs=[
            pl.BlockSpec((tq, hd), col(0, cur)),
            pl.BlockSpec((tq, hd), col(heads, prev)),
            pl.BlockSpec((tq, hd), col(heads, cur)),
            pl.BlockSpec((tq, hd), col(2 * heads, prev)),
            pl.BlockSpec((tq, hd), col(2 * heads, cur)),
            pl.BlockSpec((1, tq, tq), lambda h, b, i: (h, 0, 0)),
            pl.BlockSpec((1, tq, tq), lambda h, b, i: (h, 0, 0)),
        ],
        out_specs=pl.BlockSpec((tq, hd), lambda h, b, i: (b * nt + i, h)),
        out_shape=jax.ShapeDtypeStruct((t, heads * hd), BF16),
        scratch_shapes=[
            pltpu.VMEM((tq, tq + ATTN_ROWS), F32),
            pltpu.VMEM((tq, tq + ATTN_ROWS), BF16),
            pltpu.VMEM((tq, LANES), F32),
        ],
        compiler_params=_params(("arbitrary", "arbitrary", "arbitrary")),
        name="chunk_attention",
    )(qkv, qkv, qkv, qkv, qkv, bias_prev, bias_cur)


def _sort_pairs(n):
    pairs = []
    p = 1
    while p < n:
        k = p
        while k >= 1:
            for j in range(k % p, n - k, 2 * k):
                for i in range(min(k, n - j - k)):
                    if (i + j) // (2 * p) == (i + j + k) // (2 * p):
                        pairs.append((i + j, i + j + k))
            k //= 2
        p *= 2
    return pairs


def _cmp_exchange(x, y):
    if x is None:
        return y, None
    if y is None:
        return x, None
    return jnp.maximum(x, y), jnp.minimum(x, y)


def _opt_max(x, y):
    if x is None:
        return y
    if y is None:
        return x
    return jnp.maximum(x, y)


def _merge_top(a, a_next, b, b_next):
    k = len(a)
    hi, lo_max = [], None
    for i in range(k):
        h, l = _cmp_exchange(a[i], b[k - 1 - i])
        hi.append(h)
        lo_max = _opt_max(lo_max, l)
    d = k // 2
    while d >= 1:
        for i in range(k):
            if i & d == 0:
                hi[i], hi[i + d] = _cmp_exchange(hi[i], hi[i + d])
        d //= 2
    return hi, _opt_max(_opt_max(lo_max, a_next), b_next)


def _top_sorted(vals, k):
    vals = list(vals) + [None] * (-len(vals) % k)
    pairs = _sort_pairs(k)
    groups = []
    for g in range(0, len(vals), k):
        v = vals[g:g + k]
        for i, j in pairs:
            v[i], v[j] = _cmp_exchange(v[i], v[j])
        groups.append((v, None))
    while len(groups) > 1:
        merged = [_merge_top(*groups[i], *groups[i + 1]) for i in range(0, len(groups) - 1, 2)]
        groups = merged + groups[len(groups) - len(groups) % 2:]
    return groups[0]


def _peer_score_body(q_ref, keys_ref, e2_ref, f1_ref, thr_ref, km1_scr, km2_scr, stat_scr):
    k = PEER_TOPK
    nk = keys_ref.shape[1]
    half = keys_ref.shape[2]
    ts = q_ref.shape[1]
    groups = ts // LANES
    assert groups == SUBLANES, "one sublane per 128-token group"

    def head(h, carry):
        base = pl.multiple_of(h * 2 * half, 2 * half)
        s1 = jnp.dot(keys_ref[0], q_ref[pl.ds(base, half), :], preferred_element_type=F32)
        s2 = jnp.dot(keys_ref[1], q_ref[pl.ds(base + half, half), :], preferred_element_type=F32)
        for g in range(groups):
            km1_scr[g * nk:(g + 1) * nk, :] = s1[:, g * LANES:(g + 1) * LANES]
            km2_scr[g * nk:(g + 1) * nk, :] = s2[:, g * LANES:(g + 1) * LANES]
        v1 = [km1_scr[pl.ds(key, groups, stride=nk), :] for key in range(nk)]
        v2 = [km2_scr[pl.ds(key, groups, stride=nk), :] for key in range(nk)]
        a, a_next = _top_sorted(v1, k)
        b, b_next = _top_sorted(v2, k)
        cand = [a[i] + b[j] for i in range(k) for j in range(k) if (i + 1) * (j + 1) <= k]
        cand += [a_next + b[0], a[0] + b_next]
        best, nxt = _top_sorted(cand, k)
        z = 1.0 + jnp.exp(best[1] - best[0])
        for r in range(2, k):
            z = z + jnp.exp(best[r] - best[0])
        stat_scr[0 * groups:1 * groups, :] = a[0]
        stat_scr[1 * groups:2 * groups, :] = b[0]
        stat_scr[2 * groups:3 * groups, :] = z
        stat_scr[3 * groups:4 * groups, :] = 0.5 * (best[k - 1] + nxt)
        for g in range(groups):
            cols = slice(g * LANES, (g + 1) * LANES)
            m1 = stat_scr[0 * groups + g:0 * groups + g + 1, :]
            m2 = stat_scr[1 * groups + g:1 * groups + g + 1, :]
            zg = stat_scr[2 * groups + g:2 * groups + g + 1, :]
            tau = stat_scr[3 * groups + g:3 * groups + g + 1, :]
            s1g = km1_scr[g * nk:(g + 1) * nk, :]
            s2g = km2_scr[g * nk:(g + 1) * nk, :]
            e2_ref[h, :, cols] = jnp.exp(s2g - m2).astype(BF16)
            f1_ref[h, :, cols] = jnp.exp(s1g - m1) / zg
            thr_ref[h, :, cols] = jnp.exp(tau - s1g - m2)
        return carry

    lax.fori_loop(0, PEER_HEADS, head, 0)


def _peer_scores(qt, keys):
    qd, t = qt.shape
    ts = PEER_TS
    nk = keys.shape[1]
    groups = ts // LANES
    spec = pl.BlockSpec((PEER_HEADS, nk, ts), lambda i: (0, 0, i))
    shape = jax.ShapeDtypeStruct((PEER_HEADS, nk, t), F32)
    return pl.pallas_call(
        _peer_score_body,
        grid=(t // ts,),
        in_specs=[
            pl.BlockSpec((qd, ts), lambda i: (0, i)),
            pl.BlockSpec((2, nk, PEER_HALF), lambda i: (0, 0, 0)),
        ],
        out_specs=[spec, spec, spec],
        out_shape=[jax.ShapeDtypeStruct(shape.shape, BF16), shape, shape],
        scratch_shapes=[
            pltpu.VMEM((groups * nk, LANES), F32),
            pltpu.VMEM((groups * nk, LANES), F32),
            pltpu.VMEM((4 * groups, LANES), F32),
        ],
        compiler_params=_params(("arbitrary",)),
        name="peer_scores",
    )(qt, keys)


def _peer_stage_rows(thr_ref, f1_ref, rows_ref, base, blk, n_rows):
    last_row = thr_ref.shape[1] - 1
    for jj in range(n_rows):
        row = jnp.clip(blk * n_rows + jj, 0, last_row)
        for h in range(PEER_HEADS):
            at = base + (jj * 2) * PEER_HEADS + h
            rows_ref[at:at + 1, :] = thr_ref[h, pl.ds(row, 1), :]
            rows_ref[at + PEER_HEADS:at + PEER_HEADS + 1, :] = f1_ref[h, pl.ds(row, 1), :]


def _peer_gate_piece(e2_ref, rows_ref, base, st_ref, wp_ref, jj, ls):
    nk = e2_ref.shape[1]
    rc_n = PEER_RC
    width = ls.stop - ls.start
    at = base + (jj * 2) * PEER_HEADS
    thr = [jnp.broadcast_to(rows_ref[at + h:at + h + 1, ls], (rc_n, width)).astype(BF16) for h in range(PEER_HEADS)]
    f1 = [jnp.broadcast_to(rows_ref[at + PEER_HEADS + h:at + PEER_HEADS + h + 1, ls], (rc_n, width)).astype(BF16)
          for h in range(PEER_HEADS)]
    zero = jnp.zeros((rc_n, width), BF16)
    for rc in range(nk // rc_n):
        rs = slice(rc * rc_n, (rc + 1) * rc_n)
        wg = None
        for h in range(PEER_HEADS):
            e = e2_ref[h, rs, ls]
            part = jnp.where(e >= thr[h], e, zero) * f1[h]
            wg = part if wg is None else wg + part
        es = slice(jj * nk + rc * rc_n, jj * nk + (rc + 1) * rc_n)
        wp_ref[es, ls] = wg * _gelu(st_ref[es, ls]).astype(BF16)


def _peer_half_step(xb_ref, u_ref, vt_ref, cols, e2_ref, rows_ref, base,
                    st_out_ref, st_in_ref, wp_out_ref, wp_in_ref, acc_ref):
    nk = e2_ref.shape[1]
    n_rows = st_in_ref.shape[0] // nk
    width = PEER_GATE_LANES
    n_pc = st_in_ref.shape[1] // width
    rows_per = acc_ref.shape[0] // n_pc
    contract_last = (((1,), (1,)), ((), ()))
    for pc in range(n_pc):
        ls = slice(pc * width, (pc + 1) * width)
        st_out_ref[:, ls] = lax.dot_general(u_ref[0, cols, :], xb_ref[ls, :], contract_last,
                                            preferred_element_type=F32)
        for jj in range(n_rows):
            _peer_gate_piece(e2_ref, rows_ref, base, st_in_ref, wp_out_ref, jj, ls)
        dr = slice(pc * rows_per, (pc + 1) * rows_per)
        acc_ref[dr, :] += jnp.dot(vt_ref[0, dr, cols], wp_in_ref[...], preferred_element_type=F32)


def _peer_dense_body(alpha, xb_ref, u_ref, vt_ref, e2_ref, f1_ref, thr_ref, res_ref, g_ref, b_ref,
                     h_ref, hb_ref, acc_ref, st0_ref, st1_ref, wp0_ref, wp1_ref, rows_ref):
    j = pl.program_id(1)
    nk = e2_ref.shape[1]
    half = u_ref.shape[1] // 2
    n_rows = half // nk

    @pl.when(j == 0)
    def _():
        acc_ref[...] = jnp.zeros(acc_ref.shape, F32)
        st1_ref[...] = jnp.zeros(st1_ref.shape, F32)
        wp0_ref[...] = jnp.zeros(wp0_ref.shape, BF16)

    per_half = n_rows * 2 * PEER_HEADS
    _peer_stage_rows(thr_ref, f1_ref, rows_ref, 0, 2 * j - 1, n_rows)
    _peer_stage_rows(thr_ref, f1_ref, rows_ref, per_half, 2 * j, n_rows)

    _peer_half_step(xb_ref, u_ref, vt_ref, slice(0, half), e2_ref, rows_ref, 0,
                    st0_ref, st1_ref, wp1_ref, wp0_ref, acc_ref)
    _peer_half_step(xb_ref, u_ref, vt_ref, slice(half, 2 * half), e2_ref, rows_ref, per_half,
                    st1_ref, st0_ref, wp0_ref, wp1_ref, acc_ref)

    @pl.when(j == pl.num_programs(1) - 1)
    def _():
        y = _layer_norm(alpha * res_ref[...] + acc_ref[...].T, g_ref[...], b_ref[...])
        h_ref[...] = y
        hb_ref[...] = y.astype(BF16)


def _peer_dense(xb, u, vt, e2, f1, thr, res, g, b, alpha):
    t, d = xb.shape
    nj, eb, _ = u.shape
    nk = e2.shape[1]
    tt = min(PEER_TT, t)
    hds = PEER_HEADS
    once = pl.Buffered(1)
    return pl.pallas_call(
        functools.partial(_peer_dense_body, alpha),
        grid=(t // tt, nj + 1),
        in_specs=[
            pl.BlockSpec((tt, d), lambda i, j: (i, 0)),
            pl.BlockSpec((1, eb, d), lambda i, j: (jnp.minimum(j, nj - 1), 0, 0)),
            pl.BlockSpec((1, d, eb), lambda i, j: (jnp.maximum(j - 1, 0), 0, 0)),
            pl.BlockSpec((hds, nk, tt), lambda i, j: (0, 0, i)),
            pl.BlockSpec((hds, nk, tt), lambda i, j: (0, 0, i), pipeline_mode=once),
            pl.BlockSpec((hds, nk, tt), lambda i, j: (0, 0, i), pipeline_mode=once),
            pl.BlockSpec((tt, d), lambda i, j: (i, 0), pipeline_mode=once),
            pl.BlockSpec((1, d), lambda i, j: (0, 0)),
            pl.BlockSpec((1, d), lambda i, j: (0, 0)),
        ],
        out_specs=[pl.BlockSpec((tt, d), lambda i, j: (i, 0)), pl.BlockSpec((tt, d), lambda i, j: (i, 0))],
        out_shape=[jax.ShapeDtypeStruct((t, d), F32), jax.ShapeDtypeStruct((t, d), BF16)],
        scratch_shapes=[
            pltpu.VMEM((d, tt), F32),
            pltpu.VMEM((eb // 2, tt), F32),
            pltpu.VMEM((eb // 2, tt), F32),
            pltpu.VMEM((eb // 2, tt), BF16),
            pltpu.VMEM((eb // 2, tt), BF16),
            pltpu.VMEM((2 * (eb // 2 // nk) * 2 * hds, tt), F32),
        ],
        compiler_params=_params(("arbitrary", "arbitrary")),
        name="peer_dense",
    )(xb, u, vt, e2, f1, thr, res, g.reshape(1, d), b.reshape(1, d))


def _expert_blocks(table, transpose):
    ne, d = table.shape
    blocks = table.astype(BF16).reshape(ne // PEER_EB, PEER_EB, d)
    return blocks.transpose(0, 2, 1) if transpose else blocks


def kernel(x, p, ln_mix_g, ln_mix_b, ln_ffn_g, ln_ffn_b, pool_w_in, pool_w_group, pool_scale, pool_w_out, ssm_w_in, ssm_lambda_re, ssm_lambda_im, ssm_log_step, ssm_b_re, ssm_b_im, ssm_c_re, ssm_c_im, ssm_d, ssm_w_glu, ssm_w_out, attn_w_qkv, attn_rel_bias, attn_w_out, peer_w_q, peer_sub_keys, peer_u, peer_v, ple_w_proj, ple_w_gate, ple_b_gate):
    bsz, seq, d = x.shape
    depth = p.shape[0]
    t = bsz * seq
    alpha = (2.0 * depth) ** 0.25
    h = x.reshape(t, d)
    hb = h.astype(BF16)
    for i in range(depth):
        kind, j = i % N_MIXERS, i // N_MIXERS
        if kind == 0:
            mixed = _pool_mixer(hb, seq, pool_w_in[j].astype(BF16), pool_w_group[j].astype(BF16), pool_scale[j])
            w_out = pool_w_out[j]
        elif kind == 1:
            ops = _ssm_operators(ssm_lambda_re[j], ssm_lambda_im[j], ssm_log_step[j], ssm_b_re[j], ssm_b_im[j],
                                 ssm_c_re[j], ssm_c_im[j], ssm_d[j], SSM_LC, seq // SSM_LC)
            mixed = _ssm_mixer(hb, bsz, seq, ssm_w_in[j].astype(BF16), ops, ssm_w_glu[j].astype(BF16))
            w_out = ssm_w_out[j]
        else:
            qkv = _matmul(hb, attn_w_qkv[j].astype(BF16), BF16)
            bias_prev, bias_cur = _attn_tables(attn_rel_bias[j], ATTN_TQ)
            mixed = _chunk_attention(qkv, bsz, seq, ATTN_HEADS, bias_prev, bias_cur)
            w_out = attn_w_out[j]
        h, hb, ht = _matmul_ln(mixed, w_out.astype(BF16), h, ln_mix_g[i], ln_mix_b[i], alpha)
        qt = _matmul(peer_w_q[i].T.astype(BF16), ht, BF16)
        e2, f1, thr = _peer_scores(qt, peer_sub_keys[i].astype(BF16))
        h, hb = _peer_dense(hb, _expert_blocks(peer_u[i], False), _expert_blocks(peer_v[i], True), e2, f1, thr,
                            h, ln_ffn_g[i], ln_ffn_b[i], alpha)
        h, hb = _ple(hb, h, p[i].reshape(t, -1).astype(BF16), ple_w_gate[i].astype(BF16),
                     ple_w_proj[i].astype(BF16), ple_b_gate[i])
    return h.reshape(bsz, seq, d)
```
